```python
import math
import jax, jax.numpy as jnp
from jax import lax
import numpy as np

D_MODEL = 2048
BATCH = 4
SEQ = 4096
DEPTH = 1

CHUNK = 64
EPS = 1e-6
A_HEAD_DIM = 64
A_WIDTH = D_MODEL // 2
A_HEADS = A_WIDTH // A_HEAD_DIM
A_LEFT_CHUNKS = 8
A_BAND = (A_LEFT_CHUNKS + 1) * CHUNK
MAX_REL = 128
N_REL = 2 * MAX_REL + 1
B_WIDTH = D_MODEL - A_WIDTH
B_GROUPS = 8
B_GROUP_DIM = B_WIDTH // B_GROUPS
B_BLOCK = 128
IN_WIDTH = 3 * A_WIDTH + 2 * B_WIDTH
D_FF = 5504
N_MOD = 9

kernel_name = "hybrid_chunk_attn_gmlp_macaron_block"


def rms_norm(x, g):
    xf = x.astype(jnp.float32)
    y = xf * lax.rsqrt(jnp.mean(xf * xf, axis=-1, keepdims=True) + EPS)
    return (y * g.astype(jnp.float32)).astype(x.dtype)


def layer_norm(x, g, b):
    xf = x.astype(jnp.float32)
    mu = jnp.mean(xf, axis=-1, keepdims=True)
    xc = xf - mu
    y = xc * lax.rsqrt(jnp.mean(xc * xc, axis=-1, keepdims=True) + EPS)
    return (y * g.astype(jnp.float32) + b.astype(jnp.float32)).astype(x.dtype)


def modulate(h, shift, scale):
    return h * (1.0 + scale) + shift


def swiglu(h, w_gu, w_down):
    gu = h @ w_gu
    g, u = jnp.split(gu, 2, axis=-1)
    return (jax.nn.silu(g) * u) @ w_down


def chunk_band_attention(q, k, v, rel_bias):
    bsz, seq, nh, dh = q.shape
    nc = seq // CHUNK
    qc = q.reshape(bsz, nc, CHUNK, nh, dh)
    pad = ((0, 0), (A_LEFT_CHUNKS, 0), (0, 0), (0, 0), (0, 0))
    kp = jnp.pad(k.reshape(bsz, nc, CHUNK, nh, dh), pad)
    vp = jnp.pad(v.reshape(bsz, nc, CHUNK, nh, dh), pad)
    kb = jnp.concatenate([kp[:, w:w + nc] for w in range(A_LEFT_CHUNKS + 1)], axis=2)
    vb = jnp.concatenate([vp[:, w:w + nc] for w in range(A_LEFT_CHUNKS + 1)], axis=2)
    scores = jnp.einsum('bcqhd,bckhd->bhcqk', qc, kb).astype(jnp.float32) / math.sqrt(dh)
    qi = np.arange(CHUNK)[:, None]
    kj = np.arange(A_BAND)[None, :]
    dist = qi - (kj - A_LEFT_CHUNKS * CHUNK)
    rel_idx = np.clip(dist, -MAX_REL, MAX_REL) + MAX_REL
    bias = rel_bias.astype(jnp.float32)[:, rel_idx]
    key_chunk = np.arange(nc)[:, None] - A_LEFT_CHUNKS + (np.arange(A_BAND) // CHUNK)[None, :]
    valid = key_chunk >= 0
    scores = scores + bias[:, None, :, :]
    scores = jnp.where(valid[None, None, :, None, :], scores, jnp.float32(-1e30))
    probs = jax.nn.softmax(scores, axis=-1).astype(v.dtype)
    out = jnp.einsum('bhcqk,bckhd->bcqhd', probs, vb)
    return out.reshape(bsz, seq, nh * dh)


def chunk_spatial_gating(u, v, ln_g, ln_b, w_s, b_s):
    bsz, seq, _ = u.shape
    nb = seq // B_BLOCK
    u = jax.nn.gelu(u, approximate=False)
    v = layer_norm(jax.nn.gelu(v, approximate=False), ln_g, ln_b)
    vb = v.reshape(bsz, nb, B_BLOCK, B_GROUPS, B_GROUP_DIM)
    t = np.arange(B_BLOCK)
    mask = (t[:, None] // CHUNK) >= (t[None, :] // CHUNK)
    w = jnp.where(mask[None], w_s, jnp.zeros_like(w_s))
    z = jnp.einsum('gts,bnsgc->bntgc', w, vb) + jnp.transpose(b_s)[None, None, :, :, None]
    return u * z.reshape(bsz, seq, B_WIDTH)


def setup_inputs(seed: int = 0) -> dict:
    key = jax.random.key(seed)
    ks = jax.random.split(key, 24)
    f32 = jnp.float32

    def nrm(k, shape, scale):
        return jax.random.normal(k, shape, f32) * scale

    def gain(k, shape):
        return 1.0 + 0.1 * jax.random.normal(k, shape, f32)

    L, D = DEPTH, D_MODEL
    return {
        "x": nrm(ks[0], (BATCH, SEQ, D), 1.0),
        "c": nrm(ks[1], (BATCH, D), 1.0),
        "w_ada": nrm(ks[2], (L, D, N_MOD * D), 0.5 * D ** -0.5),
        "b_ada": nrm(ks[3], (L, N_MOD * D), 0.02),
        "ffn1_pre_g": gain(ks[4], (L, D)),
        "ffn1_post_g": gain(ks[5], (L, D)),
        "ffn1_w_gu": nrm(ks[6], (L, D, 2 * D_FF), D ** -0.5),
        "ffn1_w_down": nrm(ks[7], (L, D_FF, D), D_FF ** -0.5),
        "mix_pre_g": gain(ks[8], (L, D)),
        "mix_post_g": gain(ks[9], (L, D)),
        "w_in": nrm(ks[10], (L, D, IN_WIDTH), D ** -0.5),
        "rel_bias": nrm(ks[11], (L, A_HEADS, N_REL), 0.5),
        "ln_v_g": gain(ks[12], (L, B_WIDTH)),
        "ln_v_b": nrm(ks[13], (L, B_WIDTH), 0.02),
        "w_s": nrm(ks[14], (L, B_GROUPS, B_BLOCK, B_BLOCK), B_BLOCK ** -0.5),
        "b_s": gain(ks[15], (L, B_GROUPS, B_BLOCK)),
        "g_out_a": gain(ks[16], (L, A_WIDTH)),
        "g_out_b": gain(ks[17], (L, B_WIDTH)),
        "w_out": nrm(ks[18], (L, D, D), D ** -0.5),
        "ffn2_pre_g": gain(ks[19], (L, D)),
        "ffn2_post_g": gain(ks[20], (L, D)),
        "ffn2_w_gu": nrm(ks[21], (L, D, 2 * D_FF), D ** -0.5),
        "ffn2_w_down": nrm(ks[22], (L, D_FF, D), D_FF ** -0.5),
    }


def reference(x, c, w_ada, b_ada, ffn1_pre_g, ffn1_post_g, ffn1_w_gu, ffn1_w_down,
              mix_pre_g, mix_post_g, w_in, rel_bias, ln_v_g, ln_v_b, w_s, b_s,
              g_out_a, g_out_b, w_out, ffn2_pre_g, ffn2_post_g, ffn2_w_gu, ffn2_w_down):
    bsz, seq, _ = x.shape
    c_act = jax.nn.silu(c)
    for l in range(DEPTH):
        mod = (c_act @ w_ada[l] + b_ada[l]).reshape(bsz, N_MOD, D_MODEL)[:, :, None, :]
        sh1, sc1, gt1 = mod[:, 0], mod[:, 1], mod[:, 2]
        sh2, sc2, gt2 = mod[:, 3], mod[:, 4], mod[:, 5]
        sh3, sc3, gt3 = mod[:, 6], mod[:, 7], mod[:, 8]

        h = modulate(rms_norm(x, ffn1_pre_g[l]), sh1, sc1)
        y = swiglu(h, ffn1_w_gu[l], ffn1_w_down[l])
        x = x + 0.5 * gt1 * rms_norm(y, ffn1_post_g[l])

        h = modulate(rms_norm(x, mix_pre_g[l]), sh2, sc2)
        proj = h @ w_in[l]
        q, k, v, u_b, v_b = jnp.split(proj, [A_WIDTH, 2 * A_WIDTH, 3 * A_WIDTH,
                                             3 * A_WIDTH + B_WIDTH], axis=-1)
        hd = (bsz, seq, A_HEADS, A_HEAD_DIM)
        out_a = chunk_band_attention(q.reshape(hd), k.reshape(hd), v.reshape(hd), rel_bias[l])
        out_b = chunk_spatial_gating(u_b, v_b, ln_v_g[l], ln_v_b[l], w_s[l], b_s[l])
        merged = jnp.concatenate([rms_norm(out_a, g_out_a[l]), rms_norm(out_b, g_out_b[l])], axis=-1)
        y = merged @ w_out[l]
        x = x + gt2 * rms_norm(y, mix_post_g[l])

        h = modulate(rms_norm(x, ffn2_pre_g[l]), sh3, sc3)
        y = swiglu(h, ffn2_w_gu[l], ffn2_w_down[l])
        x = x + 0.5 * gt3 * rms_norm(y, ffn2_post_g[l])
    return x
```

```python
import functools
import math

import jax
import jax.numpy as jnp
import numpy as np
from jax import lax
from jax.experimental import pallas as pl
from jax.experimental.pallas import tpu as pltpu

F32 = jnp.float32
BF16 = jnp.bfloat16

EPS = 1e-6
CHUNK = 64
LEFT_CHUNKS = 8
HEAD_DIM = 64
MAX_REL = 128
GATE_BLOCK = 128
N_MOD = 9

LANES = 128
VMEM_LIMIT = 56 * 1024 * 1024

FFN_TM = 512
FFN_TF = 512
PROJ_TM = 1024
ATT_QB = 512
ATT_SUB = 128
ATT_KW = ATT_SUB + LEFT_CHUNKS * CHUNK
GATE_TM = 512
OUT_TM = 512
NEG = -1e30


def _params(sem, vmem=VMEM_LIMIT):
    return pltpu.CompilerParams(dimension_semantics=sem, vmem_limit_bytes=vmem)


def _rms(v):
    return v * lax.rsqrt(jnp.mean(v * v, axis=-1, keepdims=True) + EPS)


def _gelu(v):
    return 0.5 * v * (1.0 + lax.erf(v * (1.0 / math.sqrt(2.0))))


def _ada_kernel(c_ref, w_ref, b_ref, o_ref):
    c = c_ref[...]
    ca = (c * jax.nn.sigmoid(c)).astype(BF16)
    o_ref[...] = jnp.dot(ca, w_ref[...].astype(BF16), preferred_element_type=F32) + b_ref[...]


def _ada(c, w, b):
    bsz, d = c.shape
    n = w.shape[1]
    rows = 8
    cp = jnp.zeros((rows, d), F32).at[:bsz].set(c)
    tn = 1024
    out = pl.pallas_call(
        _ada_kernel,
        grid=(n // tn,),
        in_specs=[pl.BlockSpec((rows, d), lambda j: (0, 0)),
                  pl.BlockSpec((d, tn), lambda j: (0, j)),
                  pl.BlockSpec((1, tn), lambda j: (0, j))],
        out_specs=pl.BlockSpec((rows, tn), lambda j: (0, j)),
        out_shape=jax.ShapeDtypeStruct((rows, n), F32),
        compiler_params=_params(("arbitrary",)),
        name="adaln",
    )(cp, w, b.reshape(1, n))
    return out[:bsz]


def _ffn_kernel(x_ref, sh_ref, sc_ref, gt_ref, pre_ref, post_ref, wg_ref, wu_ref, wd_ref,
                o_ref, h_ref, acc_ref):
    f = pl.program_id(1)

    @pl.when(f == 0)
    def _():
        h = _rms(x_ref[...]) * pre_ref[...]
        h_ref[...] = (h * (1.0 + sc_ref[0]) + sh_ref[0]).astype(BF16)

    h = h_ref[...]
    g = jnp.dot(h, wg_ref[...], preferred_element_type=F32)
    u = jnp.dot(h, wu_ref[...], preferred_element_type=F32)
    a = (g * jax.nn.sigmoid(g) * u).astype(BF16)
    d = jnp.dot(a, wd_ref[...], preferred_element_type=F32)

    @pl.when(f == 0)
    def _():
        acc_ref[...] = d

    @pl.when(f > 0)
    def _():
        acc_ref[...] += d

    @pl.when(f == pl.num_programs(1) - 1)
    def _():
        y = _rms(acc_ref[...]) * post_ref[...]
        o_ref[...] = x_ref[...] + 0.5 * gt_ref[0] * y


def _ffn(x2, seq, sh, sc, gt, pre_g, post_g, wg, wu, wd):
    t, d = x2.shape
    fp = wg.shape[1]
    tm, tf = min(FFN_TM, seq), FFN_TF
    per_b = seq // tm
    mod_spec = pl.BlockSpec((1, 1, d), lambda i, f: (i // per_b, 0, 0))
    vec_spec = pl.BlockSpec((1, d), lambda i, f: (0, 0))
    return pl.pallas_call(
        _ffn_kernel,
        grid=(t // tm, fp // tf),
        in_specs=[pl.BlockSpec((tm, d), lambda i, f: (i, 0)),
                  mod_spec, mod_spec, mod_spec, vec_spec, vec_spec,
                  pl.BlockSpec((d, tf), lambda i, f: (0, f)),
                  pl.BlockSpec((d, tf), lambda i, f: (0, f)),
                  pl.BlockSpec((tf, d), lambda i, f: (f, 0))],
        out_specs=pl.BlockSpec((tm, d), lambda i, f: (i, 0)),
        out_shape=jax.ShapeDtypeStruct((t, d), F32),
        scratch_shapes=[pltpu.VMEM((tm, d), BF16), pltpu.VMEM((tm, d), F32)],
        compiler_params=_params(("parallel", "arbitrary")),
        name="ffn",
    )(x2, sh, sc, gt, pre_g, post_g, wg, wu, wd)


def _proj_kernel(x_ref, sh_ref, sc_ref, pre_ref, lng_ref, lnb_ref, w_ref, o_ref, h_ref):
    j = pl.program_id(1)

    @pl.when(j == 0)
    def _():
        h = _rms(x_ref[...]) * pre_ref[...]
        h_ref[...] = (h * (1.0 + sc_ref[0]) + sh_ref[0]).astype(BF16)

    r = jnp.dot(h_ref[...], w_ref[...], preferred_element_type=F32)
    n_slab = o_ref.shape[0]

    def put(val):
        vb = val.astype(BF16)
        for p in range(n_slab):
            o_ref[p] = vb[:, p * LANES:(p + 1) * LANES]

    @pl.when(j == 0)
    def _():
        put(r * (1.0 / math.sqrt(HEAD_DIM)))

    @pl.when((j == 1) | (j == 2))
    def _():
        put(r)

    @pl.when(j == 3)
    def _():
        put(_gelu(r))

    @pl.when(j == 4)
    def _():
        ge = _gelu(r)
        xc = ge - jnp.mean(ge, axis=-1, keepdims=True)
        y = xc * lax.rsqrt(jnp.mean(xc * xc, axis=-1, keepdims=True) + EPS)
        put(y * lng_ref[...] + lnb_ref[...])


def _proj(x2, seq, sh, sc, pre_g, ln_g, ln_b, w_in):
    t, d = x2.shape
    n = w_in.shape[1]
    tn = n // 5
    n_slab = tn // LANES
    tm = min(PROJ_TM, seq)
    per_b = seq // tm
    mod_spec = pl.BlockSpec((1, 1, d), lambda i, j: (i // per_b, 0, 0))
    vec_spec = pl.BlockSpec((1, d), lambda i, j: (0, 0))
    ln_spec = pl.BlockSpec((1, tn), lambda i, j: (0, 0))
    return pl.pallas_call(
        _proj_kernel,
        grid=(t // tm, 5),
        in_specs=[pl.BlockSpec((tm, d), lambda i, j: (i, 0)),
                  mod_spec, mod_spec, vec_spec, ln_spec, ln_spec,
                  pl.BlockSpec((d, tn), lambda i, j: (0, j))],
        out_specs=pl.BlockSpec((n_slab, tm, LANES), lambda i, j: (j, i, 0)),
        out_shape=jax.ShapeDtypeStruct((5 * n_slab, t, LANES), BF16),
        scratch_shapes=[pltpu.VMEM((tm, d), BF16)],
        compiler_params=_params(("parallel", "arbitrary")),
        name="in_proj",
    )(x2, sh, sc, pre_g, ln_g, ln_b, w_in)


def _attn_kernel(q_ref, kp_ref, kc_ref, vp_ref, vc_ref, bias_ref, o_ref):
    qblk = pl.program_id(2)
    q = q_ref[0]
    k = jnp.concatenate([kp_ref[0], kc_ref[0]], axis=0)
    v = jnp.concatenate([vp_ref[0], vc_ref[0]], axis=0)
    lane = lax.broadcasted_iota(jnp.int32, (ATT_SUB, LANES), 1)
    first = lane < HEAD_DIM
    col = lax.broadcasted_iota(jnp.int32, (1, ATT_KW), 1)
    first_valid = jnp.where(qblk > 0, 0, ATT_QB)
    bias = bias_ref[...]
    for i in range(ATT_QB // ATT_SUB):
        r0 = i * ATT_SUB
        qs = q[r0:r0 + ATT_SUB].astype(F32)
        q2 = jnp.concatenate([jnp.where(first, qs, 0.0), jnp.where(first, 0.0, qs)], axis=0).astype(BF16)
        ks = k[r0:r0 + ATT_KW]
        vs = v[r0:r0 + ATT_KW]
        s = lax.dot_general(q2, ks, (((1,), (1,)), ((), ())), preferred_element_type=F32)
        s = s.reshape(2, ATT_SUB, ATT_KW) + bias
        s = jnp.where((col + r0 >= first_valid)[None], s, NEG)
        m = jnp.max(s, axis=-1, keepdims=True)
        p = jnp.exp(s - m)
        l = jnp.sum(p, axis=-1, keepdims=True)
        pv = jnp.dot(p.reshape(2 * ATT_SUB, ATT_KW).astype(BF16), vs, preferred_element_type=F32)
        pv = pv.reshape(2, ATT_SUB, LANES) / l
        o_ref[0, r0:r0 + ATT_SUB, :] = jnp.where(first, pv[0], pv[1]).astype(o_ref.dtype)


def _attn(slabs, bias, bsz, seq):
    n_pair = bias.shape[0] // 2
    t = slabs.shape[1]
    nq = seq // ATT_QB
    blk = (1, ATT_QB, LANES)

    def cur(off):
        return lambda p, b, i: (off + p, b * nq + i, 0)

    def prev(off):
        return lambda p, b, i: (off + p, b * nq + jnp.maximum(i - 1, 0), 0)

    return pl.pallas_call(
        _attn_kernel,
        grid=(n_pair, bsz, nq),
        in_specs=[pl.BlockSpec(blk, cur(0)),
                  pl.BlockSpec(blk, prev(n_pair)), pl.BlockSpec(blk, cur(n_pair)),
                  pl.BlockSpec(blk, prev(2 * n_pair)), pl.BlockSpec(blk, cur(2 * n_pair)),
                  pl.BlockSpec((2, ATT_SUB, ATT_KW), lambda p, b, i: (p, 0, 0))],
        out_specs=pl.BlockSpec(blk, lambda p, b, i: (p, b * nq + i, 0)),
        out_shape=jax.ShapeDtypeStruct((n_pair, t, LANES), BF16),
        compiler_params=_params(("parallel", "parallel", "arbitrary")),
        name="chunk_attn",
    )(slabs, slabs, slabs, slabs, slabs, bias)


def _attn_bias(rel_bias):
    r = np.arange(ATT_SUB)[:, None]
    c = np.arange(ATT_KW)[None, :]
    idx = np.clip(r - c + LEFT_CHUNKS * CHUNK, -MAX_REL, MAX_REL) + MAX_REL
    lo = (r // CHUNK) * CHUNK
    in_band = (c >= lo) & (c < lo + (LEFT_CHUNKS + 1) * CHUNK)
    tab = rel_bias.astype(F32)[:, idx]
    return jnp.where(in_band[None], tab, NEG)


def _gate_kernel(u_ref, v_ref, w_ref, b_ref, o_ref):
    n_grp = u_ref.shape[0]
    tm = u_ref.shape[1]
    ti = lax.broadcasted_iota(jnp.int32, (GATE_BLOCK, GATE_BLOCK), 0)
    si = lax.broadcasted_iota(jnp.int32, (GATE_BLOCK, GATE_BLOCK), 1)
    causal = (ti // CHUNK) >= (si // CHUNK)
    for g in range(n_grp):
        w = jnp.where(causal, w_ref[g], 0.0).astype(BF16)
        for n in range(tm // GATE_BLOCK):
            rows = slice(n * GATE_BLOCK, (n + 1) * GATE_BLOCK)
            z = jnp.dot(w, v_ref[g, rows, :], preferred_element_type=F32) + b_ref[g]
            o_ref[g, rows, :] = (u_ref[g, rows, :].astype(F32) * z).astype(o_ref.dtype)


def _gate(slabs, w_s, b_s):
    n_grp = w_s.shape[0]
    t = slabs.shape[1]
    tm = GATE_TM
    b_bc = jnp.broadcast_to(b_s.astype(F32)[:, :, None], (n_grp, GATE_BLOCK, LANES))
    return pl.pallas_call(
        _gate_kernel,
        grid=(t // tm,),
        in_specs=[pl.BlockSpec((n_grp, tm, LANES), lambda i: (3, i, 0)),
                  pl.BlockSpec((n_grp, tm, LANES), lambda i: (4, i, 0)),
                  pl.BlockSpec((n_grp, GATE_BLOCK, GATE_BLOCK), lambda i: (0, 0, 0)),
                  pl.BlockSpec((n_grp, GATE_BLOCK, LANES), lambda i: (0, 0, 0))],
        out_specs=pl.BlockSpec((n_grp, tm, LANES), lambda i: (0, i, 0)),
        out_shape=jax.ShapeDtypeStruct((n_grp, t, LANES), BF16),
        compiler_params=_params(("parallel",)),
        name="spatial_gate",
    )(slabs, slabs, w_s, b_bc)


def _out_kernel(x_ref, a_ref, b_ref, ga_ref, gb_ref, gt_ref, post_ref, w_ref, o_ref, m_ref):
    n_slab = a_ref.shape[0]
    width = n_slab * LANES

    def norm_into(src_ref, g_ref, base):
        ss = None
        for p in range(n_slab):
            v = src_ref[p].astype(F32)
            part = jnp.sum(v * v, axis=-1, keepdims=True)
            ss = part if ss is None else ss + part
        inv = lax.rsqrt(ss * (1.0 / width) + EPS)
        for p in range(n_slab):
            cols = slice(base + p * LANES, base + (p + 1) * LANES)
            m_ref[:, cols] = (src_ref[p].astype(F32) * inv * g_ref[:, p * LANES:(p + 1) * LANES]).astype(BF16)

    norm_into(a_ref, ga_ref, 0)
    norm_into(b_ref, gb_ref, width)
    y = jnp.dot(m_ref[...], w_ref[...], preferred_element_type=F32)
    o_ref[...] = x_ref[...] + gt_ref[0] * (_rms(y) * post_ref[...])


def _out_proj(x2, seq, a_slabs, b_slabs, g_a, g_b, gt, post_g, w_out):
    t, d = x2.shape
    n_slab = a_slabs.shape[0]
    tm = min(OUT_TM, seq)
    per_b = seq // tm
    slab_spec = pl.BlockSpec((n_slab, tm, LANES), lambda i: (0, i, 0))
    half_spec = pl.BlockSpec((1, n_slab * LANES), lambda i: (0, 0))
    return pl.pallas_call(
        _out_kernel,
        grid=(t // tm,),
        in_specs=[pl.BlockSpec((tm, d), lambda i: (i, 0)),
                  slab_spec, slab_spec, half_spec, half_spec,
                  pl.BlockSpec((1, 1, d), lambda i: (i // per_b, 0, 0)),
                  pl.BlockSpec((1, d), lambda i: (0, 0)),
                  pl.BlockSpec((d, d), lambda i: (0, 0))],
        out_specs=pl.BlockSpec((tm, d), lambda i: (i, 0)),
        out_shape=jax.ShapeDtypeStruct((t, d), F32),
        scratch_shapes=[pltpu.VMEM((tm, d), BF16)],
        compiler_params=_params(("parallel",)),
        name="out_proj",
    )(x2, a_slabs, b_slabs, g_a, g_b, gt, post_g, w_out)


def _ffn_weights(w_gu, w_down):
    d_ff = w_down.shape[0]
    pad = (-d_ff) % FFN_TF
    wg = jnp.pad(w_gu[:, :d_ff], ((0, 0), (0, pad))).astype(BF16)
    wu = jnp.pad(w_gu[:, d_ff:], ((0, 0), (0, pad))).astype(BF16)
    wd = jnp.pad(w_down, ((0, pad), (0, 0))).astype(BF16)
    return wg, wu, wd


def kernel(x, c, w_ada, b_ada, ffn1_pre_g, ffn1_post_g, ffn1_w_gu, ffn1_w_down, mix_pre_g, mix_post_g, w_in, rel_bias, ln_v_g, ln_v_b, w_s, b_s, g_out_a, g_out_b, w_out, ffn2_pre_g, ffn2_post_g, ffn2_w_gu, ffn2_w_down):
    bsz, seq, d = x.shape
    depth = w_ada.shape[0]
    assert seq % ATT_QB == 0 and seq % min(PROJ_TM, seq) == 0
    x2 = x.reshape(bsz * seq, d)
    row = lambda v: v.reshape(1, -1)
    for l in range(depth):
        mod = _ada(c, w_ada[l], b_ada[l]).reshape(bsz, N_MOD, 1, d)
        sh1, sc1, gt1, sh2, sc2, gt2, sh3, sc3, gt3 = [mod[:, i] for i in range(N_MOD)]

        x2 = _ffn(x2, seq, sh1, sc1, gt1, row(ffn1_pre_g[l]), row(ffn1_post_g[l]),
                  *_ffn_weights(ffn1_w_gu[l], ffn1_w_down[l]))

        slabs = _proj(x2, seq, sh2, sc2, row(mix_pre_g[l]), row(ln_v_g[l]), row(ln_v_b[l]),
                      w_in[l].astype(BF16))
        out_a = _attn(slabs, _attn_bias(rel_bias[l]), bsz, seq)
        out_b = _gate(slabs, w_s[l], b_s[l])
        x2 = _out_proj(x2, seq, out_a, out_b, row(g_out_a[l]), row(g_out_b[l]), gt2,
                       row(mix_post_g[l]), w_out[l].astype(BF16))

        x2 = _ffn(x2, seq, sh3, sc3, gt3, row(ffn2_pre_g[l]), row(ffn2_post_g[l]),
                  *_ffn_weights(ffn2_w_gu[l], ffn2_w_down[l]))
    return x2.reshape(bsz, seq, d)
```

```python
import functools
import math

import jax
import jax.numpy as jnp
from jax import lax
from jax.experimental import pallas as pl
from jax.experimental.pallas import tpu as pltpu

F32 = jnp.float32
BF16 = jnp.bfloat16

EPS = 1e-6
CHUNK = 64
LEFT_CHUNKS = 8
HEAD_DIM = 64
MAX_REL = 128
GATE_BLOCK = 128
N_MOD = 9

LANES = 128
VMEM_LIMIT = 56 * 1024 * 1024

FFN_TM = 512
FFN_TF = 512
PROJ_TM = 1024
ATT_QB = 512
ATT_SUB = 128
ATT_KW = ATT_SUB + LEFT_CHUNKS * CHUNK
REL_ROW = ATT_KW + ATT_SUB
GATE_TM = 512
OUT_TM = 512
NEG = -1e30
LOG2E = math.log2(math.e)


def _params(sem, vmem=VMEM_LIMIT):
    return pltpu.CompilerParams(dimension_semantics=sem, vmem_limit_bytes=vmem)


def _rms(v):
    return v * lax.rsqrt(jnp.mean(v * v, axis=-1, keepdims=True) + EPS)


def _modulated(x, gain, scale, shift):
    return _rms(x) * (gain * (1.0 + scale)) + shift


def _gelu(v):
    return 0.5 * v * (1.0 + lax.erf(v * (1.0 / math.sqrt(2.0))))


def _ada_kernel(c_ref, w_ref, b_ref, o_ref):
    c = c_ref[...]
    ca = (c * jax.nn.sigmoid(c)).astype(BF16)
    o_ref[...] = jnp.dot(ca, w_ref[...].astype(BF16), preferred_element_type=F32) + b_ref[...]


def _ada(c, w, b):
    bsz, d = c.shape
    n = w.shape[1]
    rows = 8
    cp = jnp.zeros((rows, d), F32).at[:bsz].set(c)
    tn = 1024
    out = pl.pallas_call(
        _ada_kernel,
        grid=(n // tn,),
        in_specs=[pl.BlockSpec((rows, d), lambda j: (0, 0)),
                  pl.BlockSpec((d, tn), lambda j: (0, j)),
                  pl.BlockSpec((1, tn), lambda j: (0, j))],
        out_specs=pl.BlockSpec((rows, tn), lambda j: (0, j)),
        out_shape=jax.ShapeDtypeStruct((rows, n), F32),
        compiler_params=_params(("arbitrary",)),
        name="adaln",
    )(cp, w, b.reshape(1, n))
    return out[:bsz]


def _ffn_kernel(x_ref, sh_ref, sc_ref, gt_ref, pre_ref, post_ref, wg_ref, wu_ref, wd_ref,
                o_ref, h_ref, acc_ref, *, last_width):
    f = pl.program_id(1)
    last = pl.num_programs(1) - 1
    tf = wg_ref.shape[1]

    @pl.when(f == 0)
    def _():
        h_ref[...] = _modulated(x_ref[...], pre_ref[...], sc_ref[0], sh_ref[0]).astype(BF16)
        acc_ref[...] = jnp.zeros_like(acc_ref)

    def accumulate(width):
        h = h_ref[...]
        g = jnp.dot(h, wg_ref[:, :width], preferred_element_type=F32)
        u = jnp.dot(h, wu_ref[:, :width], preferred_element_type=F32)
        a = (g * jax.nn.sigmoid(g) * u).astype(BF16)
        acc_ref[...] += jnp.dot(a, wd_ref[:width, :], preferred_element_type=F32)

    @pl.when(f < last)
    def _():
        accumulate(tf)

    @pl.when(f == last)
    def _():
        accumulate(last_width)
        y = _rms(acc_ref[...]) * post_ref[...]
        o_ref[...] = x_ref[...] + 0.5 * gt_ref[0] * y


def _ffn(x2, seq, sh, sc, gt, pre_g, post_g, wg, wu, wd):
    t, d = x2.shape
    d_ff = wg.shape[1]
    tm, tf = min(FFN_TM, seq), FFN_TF
    n_f = pl.cdiv(d_ff, tf)
    per_b = seq // tm
    mod_spec = pl.BlockSpec((1, 1, d), lambda i, f: (i // per_b, 0, 0))
    vec_spec = pl.BlockSpec((1, d), lambda i, f: (0, 0))
    return pl.pallas_call(
        functools.partial(_ffn_kernel, last_width=d_ff - (n_f - 1) * tf),
        grid=(t // tm, n_f),
        in_specs=[pl.BlockSpec((tm, d), lambda i, f: (i, 0)),
                  mod_spec, mod_spec, mod_spec, vec_spec, vec_spec,
                  pl.BlockSpec((d, tf), lambda i, f: (0, f)),
                  pl.BlockSpec((d, tf), lambda i, f: (0, f)),
                  pl.BlockSpec((tf, d), lambda i, f: (f, 0))],
        out_specs=pl.BlockSpec((tm, d), lambda i, f: (i, 0)),
        out_shape=jax.ShapeDtypeStruct((t, d), F32),
        scratch_shapes=[pltpu.VMEM((tm, d), BF16), pltpu.VMEM((tm, d), F32)],
        compiler_params=_params(("parallel", "arbitrary")),
        name="ffn",
    )(x2, sh, sc, gt, pre_g, post_g, wg, wu, wd)


def _proj_kernel(x_ref, sh_ref, sc_ref, pre_ref, lng_ref, lnb_ref, w_ref, o_ref, h_ref):
    j = pl.program_id(1)

    @pl.when(j == 0)
    def _():
        h_ref[...] = _modulated(x_ref[...], pre_ref[...], sc_ref[0], sh_ref[0]).astype(BF16)

    r = jnp.dot(h_ref[...], w_ref[...], preferred_element_type=F32)
    n_slab = o_ref.shape[0]

    def put(val):
        vb = val.astype(BF16)
        for p in range(n_slab):
            o_ref[p] = vb[:, p * LANES:(p + 1) * LANES]

    @pl.when(j == 0)
    def _():
        put(r * (LOG2E / math.sqrt(HEAD_DIM)))

    @pl.when((j == 1) | (j == 2))
    def _():
        put(r)

    @pl.when(j == 3)
    def _():
        put(_gelu(r))

    @pl.when(j == 4)
    def _():
        ge = _gelu(r)
        xc = ge - jnp.mean(ge, axis=-1, keepdims=True)
        y = xc * lax.rsqrt(jnp.mean(xc * xc, axis=-1, keepdims=True) + EPS)
        put(y * lng_ref[...] + lnb_ref[...])


def _proj(x2, seq, sh, sc, pre_g, ln_g, ln_b, w_in):
    t, d = x2.shape
    n = w_in.shape[1]
    tn = n // 5
    n_slab = tn // LANES
    tm = min(PROJ_TM, seq)
    per_b = seq // tm
    mod_spec = pl.BlockSpec((1, 1, d), lambda i, j: (i // per_b, 0, 0))
    vec_spec = pl.BlockSpec((1, d), lambda i, j: (0, 0))
    ln_spec = pl.BlockSpec((1, tn), lambda i, j: (0, 0))
    return pl.pallas_call(
        _proj_kernel,
        grid=(t // tm, 5),
        in_specs=[pl.BlockSpec((tm, d), lambda i, j: (i, 0)),
                  mod_spec, mod_spec, vec_spec, ln_spec, ln_spec,
                  pl.BlockSpec((d, tn), lambda i, j: (0, j))],
        out_specs=pl.BlockSpec((n_slab, tm, LANES), lambda i, j: (j, i, 0)),
        out_shape=jax.ShapeDtypeStruct((5 * n_slab, t, LANES), BF16),
        scratch_shapes=[pltpu.VMEM((tm, d), BF16)],
        compiler_params=_params(("parallel", "arbitrary")),
        name="in_proj",
    )(x2, sh, sc, pre_g, ln_g, ln_b, w_in)


def _attn_kernel(q_ref, kp_ref, kc_ref, vp_ref, vc_ref, rel_ref, o_ref, tab_ref):
    n_sub = ATT_QB // ATT_SUB

    @pl.when((pl.program_id(1) == 0) & (pl.program_id(2) == 0))
    def _():
        r = lax.broadcasted_iota(jnp.int32, (ATT_SUB, ATT_KW), 0)
        c = lax.broadcasted_iota(jnp.int32, (ATT_SUB, ATT_KW), 1)
        lo = (r // CHUNK) * CHUNK
        in_band = (c >= lo) & (c < lo + (LEFT_CHUNKS + 1) * CHUNK)
        for h in range(2):
            base = jnp.broadcast_to(rel_ref[0, h:h + 1, :], (ATT_SUB, REL_ROW))
            toe = pltpu.roll(base, 0, 1, stride=1, stride_axis=0)[:, :ATT_KW] * LOG2E
            t0 = jnp.where(in_band, toe, NEG)
            tab_ref[0, h] = t0
            for i in range(n_sub):
                tab_ref[1 + i, h] = jnp.where(c + i * ATT_SUB >= ATT_QB, t0, NEG)

    qblk = pl.program_id(2)
    q = q_ref[0]
    k = jnp.concatenate([kp_ref[0], kc_ref[0]], axis=0)
    v = jnp.concatenate([vp_ref[0], vc_ref[0]], axis=0)
    first = lax.broadcasted_iota(jnp.int32, (ATT_SUB, LANES), 1) < HEAD_DIM
    for i in range(n_sub):
        r0 = i * ATT_SUB
        qs = q[r0:r0 + ATT_SUB].astype(F32)
        q2 = jnp.concatenate([jnp.where(first, qs, 0.0), jnp.where(first, 0.0, qs)], axis=0).astype(BF16)
        ks = k[r0:r0 + ATT_KW]
        vs = v[r0:r0 + ATT_KW]
        s = lax.dot_general(q2, ks, (((1,), (1,)), ((), ())), preferred_element_type=F32)
        s = s.reshape(2, ATT_SUB, ATT_KW) + tab_ref[jnp.where(qblk == 0, 1 + i, 0)]
        m = jnp.max(s, axis=-1, keepdims=True)
        p = jnp.exp2(s - m)
        l = jnp.sum(p, axis=-1, keepdims=True)
        pv = jnp.dot(p.reshape(2 * ATT_SUB, ATT_KW).astype(BF16), vs, preferred_element_type=F32)
        pv = pv.reshape(2, ATT_SUB, LANES) / l
        o_ref[0, r0:r0 + ATT_SUB, :] = jnp.where(first, pv[0], pv[1]).astype(o_ref.dtype)


def _attn(slabs, rel_rows, bsz, seq):
    n_pair = rel_rows.shape[0]
    t = slabs.shape[1]
    nq = seq // ATT_QB
    blk = (1, ATT_QB, LANES)

    def cur(off):
        return lambda p, b, i: (off + p, b * nq + i, 0)

    def prev(off):
        return lambda p, b, i: (off + p, b * nq + jnp.maximum(i - 1, 0), 0)

    return pl.pallas_call(
        _attn_kernel,
        grid=(n_pair, bsz, nq),
        in_specs=[pl.BlockSpec(blk, cur(0)),
                  pl.BlockSpec(blk, prev(n_pair)), pl.BlockSpec(blk, cur(n_pair)),
                  pl.BlockSpec(blk, prev(2 * n_pair)), pl.BlockSpec(blk, cur(2 * n_pair)),
                  pl.BlockSpec((1, 2, REL_ROW), lambda p, b, i: (p, 0, 0))],
        out_specs=pl.BlockSpec(blk, lambda p, b, i: (p, b * nq + i, 0)),
        out_shape=jax.ShapeDtypeStruct((n_pair, t, LANES), BF16),
        scratch_shapes=[pltpu.VMEM((1 + ATT_QB // ATT_SUB, 2, ATT_SUB, ATT_KW), F32)],
        compiler_params=_params(("arbitrary", "arbitrary", "arbitrary")),
        name="chunk_attn",
    )(slabs, slabs, slabs, slabs, slabs, rel_rows)


def _rel_rows(rel_bias):
    n_head = rel_bias.shape[0]
    rb = rel_bias.astype(F32)
    far = rb[:, 2 * MAX_REL:]
    n_far = ATT_KW - 2 * MAX_REL
    row = jnp.concatenate([jnp.broadcast_to(far, (n_head, n_far)), rb[:, :0:-1],
                           jnp.broadcast_to(far, (n_head, REL_ROW - ATT_KW))], axis=1)
    return row.reshape(n_head // 2, 2, REL_ROW)


def _gate_kernel(u_ref, v_ref, w_ref, b_ref, o_ref):
    n_grp = u_ref.shape[0]
    tm = u_ref.shape[1]
    ti = lax.broadcasted_iota(jnp.int32, (GATE_BLOCK, GATE_BLOCK), 0)
    si = lax.broadcasted_iota(jnp.int32, (GATE_BLOCK, GATE_BLOCK), 1)
    causal = (ti // CHUNK) >= (si // CHUNK)
    for g in range(n_grp):
        w = jnp.where(causal, w_ref[g], 0.0).astype(BF16)
        for n in range(tm // GATE_BLOCK):
            rows = slice(n * GATE_BLOCK, (n + 1) * GATE_BLOCK)
            z = jnp.dot(w, v_ref[g, rows, :], preferred_element_type=F32) + b_ref[g]
            o_ref[g, rows, :] = (u_ref[g, rows, :].astype(F32) * z).astype(o_ref.dtype)


def _gate(slabs, w_s, b_s):
    n_grp = w_s.shape[0]
    t = slabs.shape[1]
    tm = GATE_TM
    b_bc = jnp.broadcast_to(b_s.astype(F32)[:, :, None], (n_grp, GATE_BLOCK, LANES))
    return pl.pallas_call(
        _gate_kernel,
        grid=(t // tm,),
        in_specs=[pl.BlockSpec((n_grp, tm, LANES), lambda i: (3, i, 0)),
                  pl.BlockSpec((n_grp, tm, LANES), lambda i: (4, i, 0)),
                  pl.BlockSpec((n_grp, GATE_BLOCK, GATE_BLOCK), lambda i: (0, 0, 0)),
                  pl.BlockSpec((n_grp, GATE_BLOCK, LANES), lambda i: (0, 0, 0))],
        out_specs=pl.BlockSpec((n_grp, tm, LANES), lambda i: (0, i, 0)),
        out_shape=jax.ShapeDtypeStruct((n_grp, t, LANES), BF16),
        compiler_params=_params(("parallel",)),
        name="spatial_gate",
    )(slabs, slabs, w_s, b_bc)


def _out_kernel(x_ref, a_ref, b_ref, ga_ref, gb_ref, gt_ref, post_ref, w_ref, o_ref, m_ref):
    n_slab = a_ref.shape[0]
    width = n_slab * LANES

    def norm_into(src_ref, g_ref, base):
        ss = None
        for p in range(n_slab):
            v = src_ref[p].astype(F32)
            part = jnp.sum(v * v, axis=-1, keepdims=True)
            ss = part if ss is None else ss + part
        inv = lax.rsqrt(ss * (1.0 / width) + EPS)
        for p in range(n_slab):
            cols = slice(base + p * LANES, base + (p + 1) * LANES)
            m_ref[:, cols] = (src_ref[p].astype(F32) * inv * g_ref[:, p * LANES:(p + 1) * LANES]).astype(BF16)

    norm_into(a_ref, ga_ref, 0)
    norm_into(b_ref, gb_ref, width)
    y = jnp.dot(m_ref[...], w_ref[...], preferred_element_type=F32)
    o_ref[...] = x_ref[...] + gt_ref[0] * (_rms(y) * post_ref[...])


def _out_proj(x2, seq, a_slabs, b_slabs, g_a, g_b, gt, post_g, w_out):
    t, d = x2.shape
    n_slab = a_slabs.shape[0]
    tm = min(OUT_TM, seq)
    per_b = seq // tm
    slab_spec = pl.BlockSpec((n_slab, tm, LANES), lambda i: (0, i, 0))
    half_spec = pl.BlockSpec((1, n_slab * LANES), lambda i: (0, 0))
    return pl.pallas_call(
        _out_kernel,
        grid=(t // tm,),
        in_specs=[pl.BlockSpec((tm, d), lambda i: (i, 0)),
                  slab_spec, slab_spec, half_spec, half_spec,
                  pl.BlockSpec((1, 1, d), lambda i: (i // per_b, 0, 0)),
                  pl.BlockSpec((1, d), lambda i: (0, 0)),
                  pl.BlockSpec((d, d), lambda i: (0, 0))],
        out_specs=pl.BlockSpec((tm, d), lambda i: (i, 0)),
        out_shape=jax.ShapeDtypeStruct((t, d), F32),
        scratch_shapes=[pltpu.VMEM((tm, d), BF16)],
        compiler_params=_params(("parallel",)),
        name="out_proj",
    )(x2, a_slabs, b_slabs, g_a, g_b, gt, post_g, w_out)


def _ffn_weights(w_gu, w_down):
    d_ff = w_down.shape[0]
    return w_gu[:, :d_ff].astype(BF16), w_gu[:, d_ff:].astype(BF16), w_down.astype(BF16)


def kernel(x, c, w_ada, b_ada, ffn1_pre_g, ffn1_post_g, ffn1_w_gu, ffn1_w_down, mix_pre_g, mix_post_g, w_in, rel_bias, ln_v_g, ln_v_b, w_s, b_s, g_out_a, g_out_b, w_out, ffn2_pre_g, ffn2_post_g, ffn2_w_gu, ffn2_w_down):
    bsz, seq, d = x.shape
    depth = w_ada.shape[0]
    assert seq % ATT_QB == 0 and seq % min(PROJ_TM, seq) == 0
    x2 = x.reshape(bsz * seq, d)
    row = lambda v: v.reshape(1, -1)
    for l in range(depth):
        mod = _ada(c, w_ada[l], b_ada[l]).reshape(bsz, N_MOD, 1, d)
        sh1, sc1, gt1, sh2, sc2, gt2, sh3, sc3, gt3 = [mod[:, i] for i in range(N_MOD)]

        x2 = _ffn(x2, seq, sh1, sc1, gt1, row(ffn1_pre_g[l]), row(ffn1_post_g[l]),
                  *_ffn_weights(ffn1_w_gu[l], ffn1_w_down[l]))

        slabs = _proj(x2, seq, sh2, sc2, row(mix_pre_g[l]), row(ln_v_g[l]), row(ln_v_b[l]),
                      w_in[l].astype(BF16))
        out_a = _attn(slabs, _rel_rows(rel_bias[l]), bsz, seq)
        out_b = _gate(slabs, w_s[l], b_s[l])
        x2 = _out_proj(x2, seq, out_a, out_b, row(g_out_a[l]), row(g_out_b[l]), gt2,
                       row(mix_post_g[l]), w_out[l].astype(BF16))

        x2 = _ffn(x2, seq, sh3, sc3, gt3, row(ffn2_pre_g[l]), row(ffn2_post_g[l]),
                  *_ffn_weights(ffn2_w_gu[l], ffn2_w_down[l]))
    return x2.reshape(bsz, seq, d)
```

```python
import functools
import math

import jax
import jax.numpy as jnp
from jax import lax
from jax.experimental import pallas as pl
from jax.experimental.pallas import tpu as pltpu

F32 = jnp.float32
BF16 = jnp.bfloat16

EPS = 1e-6
CHUNK = 64
LEFT_CHUNKS = 8
HEAD_DIM = 64
MAX_REL = 128
GATE_BLOCK = 128
N_MOD = 9

LANES = 128
VMEM_LIMIT = 56 * 1024 * 1024

FFN_TM = 512
FFN_TF = 1024
PROJ_TM = 1024
ATT_QB = 512
ATT_SUB = 128
ATT_KW = ATT_SUB + LEFT_CHUNKS * CHUNK
REL_ROW = ATT_KW + ATT_SUB
GATE_TM = 512
OUT_TM = 512
NEG = -1e30
LOG2E = math.log2(math.e)


def _params(sem, vmem=VMEM_LIMIT):
    return pltpu.CompilerParams(dimension_semantics=sem, vmem_limit_bytes=vmem)


def _rms(v):
    return v * lax.rsqrt(jnp.mean(v * v, axis=-1, keepdims=True) + EPS)


def _modulated(x, gain, scale, shift):
    return _rms(x) * (gain * (1.0 + scale)) + shift


def _gelu(v):
    return 0.5 * v * (1.0 + lax.erf(v * (1.0 / math.sqrt(2.0))))


def _ada_kernel(c_ref, w_ref, b_ref, o_ref):
    c = c_ref[...]
    ca = (c * jax.nn.sigmoid(c)).astype(BF16)
    o_ref[...] = jnp.dot(ca, w_ref[...].astype(BF16), preferred_element_type=F32) + b_ref[...]


def _ada(c, w, b):
    bsz, d = c.shape
    n = w.shape[1]
    rows = 8
    cp = jnp.zeros((rows, d), F32).at[:bsz].set(c)
    tn = 1024
    out = pl.pallas_call(
        _ada_kernel,
        grid=(n // tn,),
        in_specs=[pl.BlockSpec((rows, d), lambda j: (0, 0)),
                  pl.BlockSpec((d, tn), lambda j: (0, j)),
                  pl.BlockSpec((1, tn), lambda j: (0, j))],
        out_specs=pl.BlockSpec((rows, tn), lambda j: (0, j)),
        out_shape=jax.ShapeDtypeStruct((rows, n), F32),
        compiler_params=_params(("arbitrary",)),
        name="adaln",
    )(cp, w, b.reshape(1, n))
    return out[:bsz]


def _ffn_kernel(x_ref, sh_ref, sc_ref, gt_ref, pre_ref, post_ref, wg_ref, wu_ref, wd_ref,
                o_ref, h_ref, acc_ref, *, last_width):
    f = pl.program_id(1)
    last = pl.num_programs(1) - 1
    tf = wg_ref.shape[1]

    def down_input(h, width):
        g = jnp.dot(h, wg_ref[:, :width], preferred_element_type=F32)
        u = jnp.dot(h, wu_ref[:, :width], preferred_element_type=F32)
        return (g * jax.nn.sigmoid(g) * u).astype(BF16)

    def accumulate(width):
        a = down_input(h_ref[...], width)
        acc_ref[...] += jnp.dot(a, wd_ref[:width, :], preferred_element_type=F32)

    @pl.when(f == 0)
    def _():
        h = _modulated(x_ref[...], pre_ref[...], sc_ref[0], sh_ref[0]).astype(BF16)
        h_ref[...] = h
        acc_ref[...] = jnp.dot(down_input(h, tf), wd_ref[...], preferred_element_type=F32)

    @pl.when((f > 0) & (f < last))
    def _():
        accumulate(tf)

    @pl.when(f == last)
    def _():
        a = down_input(h_ref[...], last_width)
        half = a.shape[0] // 2
        for r in (slice(0, half), slice(half, 2 * half)):
            y = acc_ref[r, :] + jnp.dot(a[r], wd_ref[:last_width, :], preferred_element_type=F32)
            o_ref[r, :] = x_ref[r, :] + (0.5 * gt_ref[0] * post_ref[...]) * _rms(y)


def _ffn(x2, seq, sh, sc, gt, pre_g, post_g, wg, wu, wd):
    t, d = x2.shape
    d_ff = wg.shape[1]
    tm, tf = min(FFN_TM, seq), FFN_TF
    n_f = pl.cdiv(d_ff, tf)
    per_b = seq // tm
    mod_spec = pl.BlockSpec((1, 1, d), lambda i, f: (i // per_b, 0, 0))
    vec_spec = pl.BlockSpec((1, d), lambda i, f: (0, 0))
    return pl.pallas_call(
        functools.partial(_ffn_kernel, last_width=d_ff - (n_f - 1) * tf),
        grid=(t // tm, n_f),
        in_specs=[pl.BlockSpec((tm, d), lambda i, f: (i, 0)),
                  mod_spec, mod_spec, mod_spec, vec_spec, vec_spec,
                  pl.BlockSpec((d, tf), lambda i, f: (0, f)),
                  pl.BlockSpec((d, tf), lambda i, f: (0, f)),
                  pl.BlockSpec((tf, d), lambda i, f: (f, 0))],
        out_specs=pl.BlockSpec((tm, d), lambda i, f: (i, 0)),
        out_shape=jax.ShapeDtypeStruct((t, d), F32),
        scratch_shapes=[pltpu.VMEM((tm, d), BF16), pltpu.VMEM((tm, d), F32)],
        compiler_params=_params(("parallel", "arbitrary")),
        name="ffn",
    )(x2, sh, sc, gt, pre_g, post_g, wg, wu, wd)


def _proj_kernel(x_ref, sh_ref, sc_ref, pre_ref, lng_ref, lnb_ref, w_ref, o_ref, h_ref):
    j = pl.program_id(1)
    n_slab = o_ref.shape[0]
    tm = x_ref.shape[0]

    def normed(r):
        h = _modulated(x_ref[r, :], pre_ref[...], sc_ref[0], sh_ref[0]).astype(BF16)
        h_ref[r, :] = h
        return h

    def emit(epilogue, lhs=lambda r: h_ref[r, :]):
        for r in (slice(0, tm // 2), slice(tm // 2, tm)):
            val = epilogue(jnp.dot(lhs(r), w_ref[...], preferred_element_type=F32)).astype(BF16)
            for p in range(n_slab):
                o_ref[p, r, :] = val[:, p * LANES:(p + 1) * LANES]

    def gelu_layer_norm(v):
        ge = _gelu(v)
        xc = ge - jnp.mean(ge, axis=-1, keepdims=True)
        y = xc * lax.rsqrt(jnp.mean(xc * xc, axis=-1, keepdims=True) + EPS)
        return y * lng_ref[...] + lnb_ref[...]

    @pl.when(j == 0)
    def _():
        emit(lambda v: v * (LOG2E / math.sqrt(HEAD_DIM)), lhs=normed)

    @pl.when((j == 1) | (j == 2))
    def _():
        emit(lambda v: v)

    @pl.when(j == 3)
    def _():
        emit(_gelu)

    @pl.when(j == 4)
    def _():
        emit(gelu_layer_norm)


def _proj(x2, seq, sh, sc, pre_g, ln_g, ln_b, w_in):
    t, d = x2.shape
    n = w_in.shape[1]
    tn = n // 5
    n_slab = tn // LANES
    tm = min(PROJ_TM, seq)
    per_b = seq // tm
    mod_spec = pl.BlockSpec((1, 1, d), lambda i, j: (i // per_b, 0, 0))
    vec_spec = pl.BlockSpec((1, d), lambda i, j: (0, 0))
    ln_spec = pl.BlockSpec((1, tn), lambda i, j: (0, 0))
    return pl.pallas_call(
        _proj_kernel,
        grid=(t // tm, 5),
        in_specs=[pl.BlockSpec((tm, d), lambda i, j: (i, 0)),
                  mod_spec, mod_spec, vec_spec, ln_spec, ln_spec,
                  pl.BlockSpec((d, tn), lambda i, j: (0, j))],
        out_specs=pl.BlockSpec((n_slab, tm, LANES), lambda i, j: (j, i, 0)),
        out_shape=jax.ShapeDtypeStruct((5 * n_slab, t, LANES), BF16),
        scratch_shapes=[pltpu.VMEM((tm, d), BF16)],
        compiler_params=_params(("parallel", "arbitrary")),
        name="in_proj",
    )(x2, sh, sc, pre_g, ln_g, ln_b, w_in)


def _attn_kernel(q_ref, kp_ref, kc_ref, vp_ref, vc_ref, rel_ref, o_ref, tab_ref):
    n_sub = ATT_QB // ATT_SUB

    @pl.when((pl.program_id(1) == 0) & (pl.program_id(2) == 0))
    def _():
        r = lax.broadcasted_iota(jnp.int32, (ATT_SUB, ATT_KW), 0)
        c = lax.broadcasted_iota(jnp.int32, (ATT_SUB, ATT_KW), 1)
        lo = (r // CHUNK) * CHUNK
        in_band = (c >= lo) & (c < lo + (LEFT_CHUNKS + 1) * CHUNK)
        for h in range(2):
            base = jnp.broadcast_to(rel_ref[0, h:h + 1, :], (ATT_SUB, REL_ROW))
            toe = pltpu.roll(base, 0, 1, stride=1, stride_axis=0)[:, :ATT_KW] * LOG2E
            t0 = jnp.where(in_band, toe, NEG)
            tab_ref[0, h] = t0
            for i in range(n_sub):
                tab_ref[1 + i, h] = jnp.where(c + i * ATT_SUB >= ATT_QB, t0, NEG)

    qblk = pl.program_id(2)
    q = q_ref[0]
    k = jnp.concatenate([kp_ref[0], kc_ref[0]], axis=0)
    v = jnp.concatenate([vp_ref[0], vc_ref[0]], axis=0)
    first = lax.broadcasted_iota(jnp.int32, (ATT_SUB, LANES), 1) < HEAD_DIM
    for i in range(n_sub):
        r0 = i * ATT_SUB
        qs = q[r0:r0 + ATT_SUB].astype(F32)
        q2 = jnp.concatenate([jnp.where(first, qs, 0.0), jnp.where(first, 0.0, qs)], axis=0).astype(BF16)
        ks = k[r0:r0 + ATT_KW]
        vs = v[r0:r0 + ATT_KW]
        s = lax.dot_general(q2, ks, (((1,), (1,)), ((), ())), preferred_element_type=F32)
        s = s.reshape(2, ATT_SUB, ATT_KW) + tab_ref[jnp.where(qblk == 0, 1 + i, 0)]
        m = jnp.max(s, axis=-1, keepdims=True)
        p = jnp.exp2(s - m)
        l = jnp.sum(p, axis=-1, keepdims=True)
        pv = jnp.dot(p.reshape(2 * ATT_SUB, ATT_KW).astype(BF16), vs, preferred_element_type=F32)
        pv = pv.reshape(2, ATT_SUB, LANES) / l
        o_ref[0, r0:r0 + ATT_SUB, :] = jnp.where(first, pv[0], pv[1]).astype(o_ref.dtype)


def _attn(slabs, rel_rows, bsz, seq):
    n_pair = rel_rows.shape[0]
    t = slabs.shape[1]
    nq = seq // ATT_QB
    blk = (1, ATT_QB, LANES)

    def cur(off):
        return lambda p, b, i: (off + p, b * nq + i, 0)

    def prev(off):
        return lambda p, b, i: (off + p, b * nq + jnp.maximum(i - 1, 0), 0)

    return pl.pallas_call(
        _attn_kernel,
        grid=(n_pair, bsz, nq),
        in_specs=[pl.BlockSpec(blk, cur(0)),
                  pl.BlockSpec(blk, prev(n_pair)), pl.BlockSpec(blk, cur(n_pair)),
                  pl.BlockSpec(blk, prev(2 * n_pair)), pl.BlockSpec(blk, cur(2 * n_pair)),
                  pl.BlockSpec((1, 2, REL_ROW), lambda p, b, i: (p, 0, 0))],
        out_specs=pl.BlockSpec(blk, lambda p, b, i: (p, b * nq + i, 0)),
        out_shape=jax.ShapeDtypeStruct((n_pair, t, LANES), BF16),
        scratch_shapes=[pltpu.VMEM((1 + ATT_QB // ATT_SUB, 2, ATT_SUB, ATT_KW), F32)],
        compiler_params=_params(("arbitrary", "arbitrary", "arbitrary")),
        name="chunk_attn",
    )(slabs, slabs, slabs, slabs, slabs, rel_rows)


def _rel_rows(rel_bias):
    n_head = rel_bias.shape[0]
    rb = rel_bias.astype(F32)
    far = rb[:, 2 * MAX_REL:]
    n_far = ATT_KW - 2 * MAX_REL
    row = jnp.concatenate([jnp.broadcast_to(far, (n_head, n_far)), rb[:, :0:-1],
                           jnp.broadcast_to(far, (n_head, REL_ROW - ATT_KW))], axis=1)
    return row.reshape(n_head // 2, 2, REL_ROW)


def _gate_kernel(u_ref, v_ref, w_ref, b_ref, o_ref):
    n_grp = u_ref.shape[0]
    tm = u_ref.shape[1]
    ti = lax.broadcasted_iota(jnp.int32, (GATE_BLOCK, GATE_BLOCK), 0)
    si = lax.broadcasted_iota(jnp.int32, (GATE_BLOCK, GATE_BLOCK), 1)
    causal = (ti // CHUNK) >= (si // CHUNK)
    for g in range(n_grp):
        w = jnp.where(causal, w_ref[g], 0.0).astype(BF16)
        for n in range(tm // GATE_BLOCK):
            rows = slice(n * GATE_BLOCK, (n + 1) * GATE_BLOCK)
            z = jnp.dot(w, v_ref[g, rows, :], preferred_element_type=F32) + b_ref[g]
            o_ref[g, rows, :] = (u_ref[g, rows, :].astype(F32) * z).astype(o_ref.dtype)


def _gate(slabs, w_s, b_s):
    n_grp = w_s.shape[0]
    t = slabs.shape[1]
    tm = GATE_TM
    b_bc = jnp.broadcast_to(b_s.astype(F32)[:, :, None], (n_grp, GATE_BLOCK, LANES))
    return pl.pallas_call(
        _gate_kernel,
        grid=(t // tm,),
        in_specs=[pl.BlockSpec((n_grp, tm, LANES), lambda i: (3, i, 0)),
                  pl.BlockSpec((n_grp, tm, LANES), lambda i: (4, i, 0)),
                  pl.BlockSpec((n_grp, GATE_BLOCK, GATE_BLOCK), lambda i: (0, 0, 0)),
                  pl.BlockSpec((n_grp, GATE_BLOCK, LANES), lambda i: (0, 0, 0))],
        out_specs=pl.BlockSpec((n_grp, tm, LANES), lambda i: (0, i, 0)),
        out_shape=jax.ShapeDtypeStruct((n_grp, t, LANES), BF16),
        compiler_params=_params(("parallel",)),
        name="spatial_gate",
    )(slabs, slabs, w_s, b_bc)


def _out_kernel(x_ref, a_ref, b_ref, ga_ref, gb_ref, gt_ref, post_ref, w_ref, o_ref):
    n_slab = a_ref.shape[0]
    width = n_slab * LANES
    tm = x_ref.shape[0]

    def normed_slabs(src_ref, g_ref, r):
        ss = None
        for p in range(n_slab):
            v = src_ref[p, r, :].astype(F32)
            part = jnp.sum(v * v, axis=-1, keepdims=True)
            ss = part if ss is None else ss + part
        inv = lax.rsqrt(ss * (1.0 / width) + EPS)
        return [(src_ref[p, r, :].astype(F32) * inv * g_ref[:, p * LANES:(p + 1) * LANES]).astype(BF16)
                for p in range(n_slab)]

    coef = gt_ref[0] * post_ref[...]
    for r in (slice(0, tm // 2), slice(tm // 2, tm)):
        merged = jnp.concatenate(normed_slabs(a_ref, ga_ref, r) + normed_slabs(b_ref, gb_ref, r), axis=1)
        y = jnp.dot(merged, w_ref[...], preferred_element_type=F32)
        o_ref[r, :] = x_ref[r, :] + coef * _rms(y)


def _out_proj(x2, seq, a_slabs, b_slabs, g_a, g_b, gt, post_g, w_out):
    t, d = x2.shape
    n_slab = a_slabs.shape[0]
    tm = min(OUT_TM, seq)
    per_b = seq // tm
    slab_spec = pl.BlockSpec((n_slab, tm, LANES), lambda i: (0, i, 0))
    half_spec = pl.BlockSpec((1, n_slab * LANES), lambda i: (0, 0))
    return pl.pallas_call(
        _out_kernel,
        grid=(t // tm,),
        in_specs=[pl.BlockSpec((tm, d), lambda i: (i, 0)),
                  slab_spec, slab_spec, half_spec, half_spec,
                  pl.BlockSpec((1, 1, d), lambda i: (i // per_b, 0, 0)),
                  pl.BlockSpec((1, d), lambda i: (0, 0)),
                  pl.BlockSpec((d, d), lambda i: (0, 0))],
        out_specs=pl.BlockSpec((tm, d), lambda i: (i, 0)),
        out_shape=jax.ShapeDtypeStruct((t, d), F32),
        compiler_params=_params(("parallel",)),
        name="out_proj",
    )(x2, a_slabs, b_slabs, g_a, g_b, gt, post_g, w_out)


def _ffn_weights(w_gu, w_down):
    d_ff = w_down.shape[0]
    return w_gu[:, :d_ff].astype(BF16), w_gu[:, d_ff:].astype(BF16), w_down.astype(BF16)


def kernel(x, c, w_ada, b_ada, ffn1_pre_g, ffn1_post_g, ffn1_w_gu, ffn1_w_down, mix_pre_g, mix_post_g, w_in, rel_bias, ln_v_g, ln_v_b, w_s, b_s, g_out_a, g_out_b, w_out, ffn2_pre_g, ffn2_post_g, ffn2_w_gu, ffn2_w_down):
    bsz, seq, d = x.shape
    depth = w_ada.shape[0]
    assert seq % ATT_QB == 0 and seq % min(PROJ_TM, seq) == 0
    x2 = x.reshape(bsz * seq, d)
    row = lambda v: v.reshape(1, -1)
    for l in range(depth):
        mod = _ada(c, w_ada[l], b_ada[l]).reshape(bsz, N_MOD, 1, d)
        sh1, sc1, gt1, sh2, sc2, gt2, sh3, sc3, gt3 = [mod[:, i] for i in range(N_MOD)]

        x2 = _ffn(x2, seq, sh1, sc1, gt1, row(ffn1_pre_g[l]), row(ffn1_post_g[l]),
                  *_ffn_weights(ffn1_w_gu[l], ffn1_w_down[l]))

        slabs = _proj(x2, seq, sh2, sc2, row(mix_pre_g[l]), row(ln_v_g[l]), row(ln_v_b[l]),
                      w_in[l].astype(BF16))
        out_a = _attn(slabs, _rel_rows(rel_bias[l]), bsz, seq)
        out_b = _gate(slabs, w_s[l], b_s[l])
        x2 = _out_proj(x2, seq, out_a, out_b, row(g_out_a[l]), row(g_out_b[l]), gt2,
                       row(mix_post_g[l]), w_out[l].astype(BF16))

        x2 = _ffn(x2, seq, sh3, sc3, gt3, row(ffn2_pre_g[l]), row(ffn2_post_g[l]),
                  *_ffn_weights(ffn2_w_gu[l], ffn2_w_down[l]))
    return x2.reshape(bsz, seq, d)
```

```python
import functools
import math

import jax
import jax.numpy as jnp
from jax import lax
from jax.experimental import pallas as pl
from jax.experimental.pallas import tpu as pltpu

F32 = jnp.float32
BF16 = jnp.bfloat16

EPS = 1e-6
CHUNK = 64
LEFT_CHUNKS = 8
HEAD_DIM = 64
MAX_REL = 128
GATE_BLOCK = 128
N_MOD = 9

LANES = 128
VMEM_LIMIT = 56 * 1024 * 1024

FFN_TM = 512
FFN_TF = 1024
PROJ_TM = 1024
ATT_QB = 512
ATT_SUB = 128
ATT_KW = ATT_SUB + LEFT_CHUNKS * CHUNK
REL_ROW = ATT_KW + ATT_SUB
GATE_TM = 512
OUT_TM = 512
NEG = -1e30
LOG2E = math.log2(math.e)


def _params(sem, vmem=VMEM_LIMIT):
    return pltpu.CompilerParams(dimension_semantics=sem, vmem_limit_bytes=vmem)


def _rms(v):
    return v * lax.rsqrt(jnp.mean(v * v, axis=-1, keepdims=True) + EPS)


def _modulated(x, gain, scale, shift):
    return _rms(x) * (gain * (1.0 + scale)) + shift


def _gelu(v):
    return 0.5 * v * (1.0 + lax.erf(v * (1.0 / math.sqrt(2.0))))


def _ada_kernel(c_ref, w_ref, b_ref, o_ref):
    c = c_ref[...]
    ca = (c * jax.nn.sigmoid(c)).astype(BF16)
    o_ref[...] = jnp.dot(ca, w_ref[...].astype(BF16), preferred_element_type=F32) + b_ref[...]


def _ada(c, w, b):
    bsz, d = c.shape
    n = w.shape[1]
    rows = 8
    cp = jnp.zeros((rows, d), F32).at[:bsz].set(c)
    tn = 1024
    out = pl.pallas_call(
        _ada_kernel,
        grid=(n // tn,),
        in_specs=[pl.BlockSpec((rows, d), lambda j: (0, 0)),
                  pl.BlockSpec((d, tn), lambda j: (0, j)),
                  pl.BlockSpec((1, tn), lambda j: (0, j))],
        out_specs=pl.BlockSpec((rows, tn), lambda j: (0, j)),
        out_shape=jax.ShapeDtypeStruct((rows, n), F32),
        compiler_params=_params(("arbitrary",)),
        name="adaln",
    )(cp, w, b.reshape(1, n))
    return out[:bsz]


def _ffn_kernel(x_ref, sh_ref, sc_ref, gt_ref, pre_ref, post_ref, wg_ref, wu_ref, wd_ref,
                o_ref, h_ref, acc_ref, *, last_width):
    step = pl.program_id(1)
    n_f = pl.num_programs(1)
    tf = wg_ref.shape[1]
    is_partial = _ffn_tile(pl.program_id(0), step, n_f) == n_f - 1

    def down_input(h, width):
        g = jnp.dot(h, wg_ref[:, :width], preferred_element_type=F32)
        u = jnp.dot(h, wu_ref[:, :width], preferred_element_type=F32)
        return (g * jax.nn.sigmoid(g) * u).astype(BF16)

    def accumulate(width):
        a = down_input(h_ref[...], width)
        acc_ref[...] += jnp.dot(a, wd_ref[:width, :], preferred_element_type=F32)

    @pl.when(step == 0)
    def _():
        h = _modulated(x_ref[...], pre_ref[...], sc_ref[0], sh_ref[0]).astype(BF16)
        h_ref[...] = h
        acc_ref[...] = jnp.dot(down_input(h, tf), wd_ref[...], preferred_element_type=F32)

    @pl.when((step > 0) & (step < n_f - 1) & jnp.logical_not(is_partial))
    def _():
        accumulate(tf)

    @pl.when(is_partial)
    def _():
        accumulate(last_width)

    @pl.when(step == n_f - 1)
    def _():
        a = down_input(h_ref[...], tf)
        half = a.shape[0] // 2
        for r in (slice(0, half), slice(half, 2 * half)):
            y = acc_ref[r, :] + jnp.dot(a[r], wd_ref[...], preferred_element_type=F32)
            o_ref[r, :] = x_ref[r, :] + (0.5 * gt_ref[0] * post_ref[...]) * _rms(y)


def _ffn_tile(i, step, n_f):
    mid = (n_f - 1) // 2
    pos = jnp.where(i % 2 == 0, step, n_f - 1 - step)
    return jnp.where(pos == mid, n_f - 1, jnp.where(pos < mid, pos, pos - 1))


def _ffn(x2, seq, sh, sc, gt, pre_g, post_g, wg, wu, wd):
    t, d = x2.shape
    d_ff = wg.shape[1]
    tm, tf = min(FFN_TM, seq), FFN_TF
    n_f = pl.cdiv(d_ff, tf)
    assert n_f >= 3
    per_b = seq // tm
    mod_spec = pl.BlockSpec((1, 1, d), lambda i, f: (i // per_b, 0, 0))
    vec_spec = pl.BlockSpec((1, d), lambda i, f: (0, 0))
    return pl.pallas_call(
        functools.partial(_ffn_kernel, last_width=d_ff - (n_f - 1) * tf),
        grid=(t // tm, n_f),
        in_specs=[pl.BlockSpec((tm, d), lambda i, f: (i, 0)),
                  mod_spec, mod_spec, mod_spec, vec_spec, vec_spec,
                  pl.BlockSpec((d, tf), lambda i, f: (0, _ffn_tile(i, f, n_f))),
                  pl.BlockSpec((d, tf), lambda i, f: (0, _ffn_tile(i, f, n_f))),
                  pl.BlockSpec((tf, d), lambda i, f: (_ffn_tile(i, f, n_f), 0))],
        out_specs=pl.BlockSpec((tm, d), lambda i, f: (i, 0)),
        out_shape=jax.ShapeDtypeStruct((t, d), F32),
        scratch_shapes=[pltpu.VMEM((tm, d), BF16), pltpu.VMEM((tm, d), F32)],
        compiler_params=_params(("parallel", "arbitrary")),
        name="ffn",
    )(x2, sh, sc, gt, pre_g, post_g, wg, wu, wd)


def _proj_kernel(x_ref, sh_ref, sc_ref, pre_ref, lng_ref, lnb_ref, w_ref, o_ref, h_ref):
    j = pl.program_id(1)
    n_slab = o_ref.shape[0]
    tm = x_ref.shape[0]

    def normed(r):
        h = _modulated(x_ref[r, :], pre_ref[...], sc_ref[0], sh_ref[0]).astype(BF16)
        h_ref[r, :] = h
        return h

    def emit(epilogue, lhs=lambda r: h_ref[r, :]):
        for r in (slice(0, tm // 2), slice(tm // 2, tm)):
            val = epilogue(jnp.dot(lhs(r), w_ref[...], preferred_element_type=F32)).astype(BF16)
            for p in range(n_slab):
                o_ref[p, r, :] = val[:, p * LANES:(p + 1) * LANES]

    def gelu_layer_norm(v):
        ge = _gelu(v)
        xc = ge - jnp.mean(ge, axis=-1, keepdims=True)
        y = xc * lax.rsqrt(jnp.mean(xc * xc, axis=-1, keepdims=True) + EPS)
        return y * lng_ref[...] + lnb_ref[...]

    @pl.when(j == 0)
    def _():
        emit(lambda v: v * (LOG2E / math.sqrt(HEAD_DIM)), lhs=normed)

    @pl.when((j == 1) | (j == 2))
    def _():
        emit(lambda v: v)

    @pl.when(j == 3)
    def _():
        emit(_gelu)

    @pl.when(j == 4)
    def _():
        emit(gelu_layer_norm)


def _proj(x2, seq, sh, sc, pre_g, ln_g, ln_b, w_in):
    t, d = x2.shape
    n = w_in.shape[1]
    tn = n // 5
    n_slab = tn // LANES
    tm = min(PROJ_TM, seq)
    per_b = seq // tm
    mod_spec = pl.BlockSpec((1, 1, d), lambda i, j: (i // per_b, 0, 0))
    vec_spec = pl.BlockSpec((1, d), lambda i, j: (0, 0))
    ln_spec = pl.BlockSpec((1, tn), lambda i, j: (0, 0))
    return pl.pallas_call(
        _proj_kernel,
        grid=(t // tm, 5),
        in_specs=[pl.BlockSpec((tm, d), lambda i, j: (i, 0)),
                  mod_spec, mod_spec, vec_spec, ln_spec, ln_spec,
                  pl.BlockSpec((d, tn), lambda i, j: (0, j))],
        out_specs=pl.BlockSpec((n_slab, tm, LANES), lambda i, j: (j, i, 0)),
        out_shape=jax.ShapeDtypeStruct((5 * n_slab, t, LANES), BF16),
        scratch_shapes=[pltpu.VMEM((tm, d), BF16)],
        compiler_params=_params(("parallel", "arbitrary")),
        name="in_proj",
    )(x2, sh, sc, pre_g, ln_g, ln_b, w_in)


def _attn_kernel(q_ref, kp_ref, kc_ref, vp_ref, vc_ref, rel_ref, o_ref, tab_ref):
    n_sub = ATT_QB // ATT_SUB

    @pl.when((pl.program_id(1) == 0) & (pl.program_id(2) == 0))
    def _():
        r = lax.broadcasted_iota(jnp.int32, (ATT_SUB, ATT_KW), 0)
        c = lax.broadcasted_iota(jnp.int32, (ATT_SUB, ATT_KW), 1)
        lo = (r // CHUNK) * CHUNK
        in_band = (c >= lo) & (c < lo + (LEFT_CHUNKS + 1) * CHUNK)
        for h in range(2):
            base = jnp.broadcast_to(rel_ref[0, h:h + 1, :], (ATT_SUB, REL_ROW))
            toe = pltpu.roll(base, 0, 1, stride=1, stride_axis=0)[:, :ATT_KW] * LOG2E
            t0 = jnp.where(in_band, toe, NEG)
            tab_ref[0, h] = t0
            for i in range(n_sub):
                tab_ref[1 + i, h] = jnp.where(c + i * ATT_SUB >= ATT_QB, t0, NEG)

    qblk = pl.program_id(2)
    q = q_ref[0]
    k = jnp.concatenate([kp_ref[0], kc_ref[0]], axis=0)
    v = jnp.concatenate([vp_ref[0], vc_ref[0]], axis=0)
    first = lax.broadcasted_iota(jnp.int32, (ATT_SUB, LANES), 1) < HEAD_DIM
    def scores(i):
        qs = q[i * ATT_SUB:(i + 1) * ATT_SUB].astype(F32)
        q2 = jnp.concatenate([jnp.where(first, qs, 0.0), jnp.where(first, 0.0, qs)], axis=0).astype(BF16)
        ks = k[i * ATT_SUB:i * ATT_SUB + ATT_KW]
        return lax.dot_general(q2, ks, (((1,), (1,)), ((), ())), preferred_element_type=F32)

    ahead = 2
    pending = [scores(i) for i in range(ahead)]
    for i in range(n_sub):
        r0 = i * ATT_SUB
        vs = v[r0:r0 + ATT_KW]
        s = pending.pop(0)
        if i + ahead < n_sub:
            pending.append(scores(i + ahead))
        s = s.reshape(2, ATT_SUB, ATT_KW) + tab_ref[jnp.where(qblk == 0, 1 + i, 0)]
        m = jnp.max(s, axis=-1, keepdims=True)
        p = jnp.exp2(s - m)
        l = jnp.sum(p, axis=-1, keepdims=True)
        pv = jnp.dot(p.reshape(2 * ATT_SUB, ATT_KW).astype(BF16), vs, preferred_element_type=F32)
        pv = pv.reshape(2, ATT_SUB, LANES) / l
        o_ref[0, r0:r0 + ATT_SUB, :] = jnp.where(first, pv[0], pv[1]).astype(o_ref.dtype)


def _attn(slabs, rel_rows, bsz, seq):
    n_pair = rel_rows.shape[0]
    t = slabs.shape[1]
    nq = seq // ATT_QB
    blk = (1, ATT_QB, LANES)

    def cur(off):
        return lambda p, b, i: (off + p, b * nq + i, 0)

    def prev(off):
        return lambda p, b, i: (off + p, b * nq + jnp.maximum(i - 1, 0), 0)

    return pl.pallas_call(
        _attn_kernel,
        grid=(n_pair, bsz, nq),
        in_specs=[pl.BlockSpec(blk, cur(0)),
                  pl.BlockSpec(blk, prev(n_pair)), pl.BlockSpec(blk, cur(n_pair)),
                  pl.BlockSpec(blk, prev(2 * n_pair)), pl.BlockSpec(blk, cur(2 * n_pair)),
                  pl.BlockSpec((1, 2, REL_ROW), lambda p, b, i: (p, 0, 0))],
        out_specs=pl.BlockSpec(blk, lambda p, b, i: (p, b * nq + i, 0)),
        out_shape=jax.ShapeDtypeStruct((n_pair, t, LANES), BF16),
        scratch_shapes=[pltpu.VMEM((1 + ATT_QB // ATT_SUB, 2, ATT_SUB, ATT_KW), F32)],
        compiler_params=_params(("arbitrary", "arbitrary", "arbitrary")),
        name="chunk_attn",
    )(slabs, slabs, slabs, slabs, slabs, rel_rows)


def _rel_rows(rel_bias):
    n_head = rel_bias.shape[0]
    rb = rel_bias.astype(F32)
    far = rb[:, 2 * MAX_REL:]
    n_far = ATT_KW - 2 * MAX_REL
    row = jnp.concatenate([jnp.broadcast_to(far, (n_head, n_far)), rb[:, :0:-1],
                           jnp.broadcast_to(far, (n_head, REL_ROW - ATT_KW))], axis=1)
    return row.reshape(n_head // 2, 2, REL_ROW)


def _gate_kernel(u_ref, v_ref, w_ref, b_ref, o_ref):
    n_grp = u_ref.shape[0]
    tm = u_ref.shape[1]
    ti = lax.broadcasted_iota(jnp.int32, (GATE_BLOCK, GATE_BLOCK), 0)
    si = lax.broadcasted_iota(jnp.int32, (GATE_BLOCK, GATE_BLOCK), 1)
    causal = (ti // CHUNK) >= (si // CHUNK)
    for g in range(n_grp):
        w = jnp.where(causal, w_ref[g], 0.0).astype(BF16)
        for n in range(tm // GATE_BLOCK):
            rows = slice(n * GATE_BLOCK, (n + 1) * GATE_BLOCK)
            z = jnp.dot(w, v_ref[g, rows, :], preferred_element_type=F32) + b_ref[g]
            o_ref[g, rows, :] = (u_ref[g, rows, :].astype(F32) * z).astype(o_ref.dtype)


def _gate(slabs, w_s, b_s):
    n_grp = w_s.shape[0]
    t = slabs.shape[1]
    tm = GATE_TM
    b_bc = jnp.broadcast_to(b_s.astype(F32)[:, :, None], (n_grp, GATE_BLOCK, LANES))
    return pl.pallas_call(
        _gate_kernel,
        grid=(t // tm,),
        in_specs=[pl.BlockSpec((n_grp, tm, LANES), lambda i: (3, i, 0)),
                  pl.BlockSpec((n_grp, tm, LANES), lambda i: (4, i, 0)),
                  pl.BlockSpec((n_grp, GATE_BLOCK, GATE_BLOCK), lambda i: (0, 0, 0)),
                  pl.BlockSpec((n_grp, GATE_BLOCK, LANES), lambda i: (0, 0, 0))],
        out_specs=pl.BlockSpec((n_grp, tm, LANES), lambda i: (0, i, 0)),
        out_shape=jax.ShapeDtypeStruct((n_grp, t, LANES), BF16),
        compiler_params=_params(("parallel",)),
        name="spatial_gate",
    )(slabs, slabs, w_s, b_bc)


def _out_kernel(x_ref, a_ref, b_ref, ga_ref, gb_ref, gt_ref, post_ref, w_ref, o_ref):
    n_slab = a_ref.shape[0]
    width = n_slab * LANES
    tm = x_ref.shape[0]

    def normed_slabs(src_ref, g_ref, r):
        ss = None
        for p in range(n_slab):
            v = src_ref[p, r, :].astype(F32)
            part = jnp.sum(v * v, axis=-1, keepdims=True)
            ss = part if ss is None else ss + part
        inv = lax.rsqrt(ss * (1.0 / width) + EPS)
        return [(src_ref[p, r, :].astype(F32) * inv * g_ref[:, p * LANES:(p + 1) * LANES]).astype(BF16)
                for p in range(n_slab)]

    coef = gt_ref[0] * post_ref[...]
    for r in (slice(0, tm // 2), slice(tm // 2, tm)):
        merged = jnp.concatenate(normed_slabs(a_ref, ga_ref, r) + normed_slabs(b_ref, gb_ref, r), axis=1)
        y = jnp.dot(merged, w_ref[...], preferred_element_type=F32)
        o_ref[r, :] = x_ref[r, :] + coef * _rms(y)


def _out_proj(x2, seq, a_slabs, b_slabs, g_a, g_b, gt, post_g, w_out):
    t, d = x2.shape
    n_slab = a_slabs.shape[0]
    tm = min(OUT_TM, seq)
    per_b = seq // tm
    slab_spec = pl.BlockSpec((n_slab, tm, LANES), lambda i: (0, i, 0))
    half_spec = pl.BlockSpec((1, n_slab * LANES), lambda i: (0, 0))
    return pl.pallas_call(
        _out_kernel,
        grid=(t // tm,),
        in_specs=[pl.BlockSpec((tm, d), lambda i: (i, 0)),
                  slab_spec, slab_spec, half_spec, half_spec,
                  pl.BlockSpec((1, 1, d), lambda i: (i // per_b, 0, 0)),
                  pl.BlockSpec((1, d), lambda i: (0, 0)),
                  pl.BlockSpec((d, d), lambda i: (0, 0))],
        out_specs=pl.BlockSpec((tm, d), lambda i: (i, 0)),
        out_shape=jax.ShapeDtypeStruct((t, d), F32),
        compiler_params=_params(("parallel",)),
        name="out_proj",
    )(x2, a_slabs, b_slabs, g_a, g_b, gt, post_g, w_out)


def _ffn_weights(w_gu, w_down):
    d_ff = w_down.shape[0]
    return w_gu[:, :d_ff].astype(BF16), w_gu[:, d_ff:].astype(BF16), w_down.astype(BF16)


def kernel(x, c, w_ada, b_ada, ffn1_pre_g, ffn1_post_g, ffn1_w_gu, ffn1_w_down, mix_pre_g, mix_post_g, w_in, rel_bias, ln_v_g, ln_v_b, w_s, b_s, g_out_a, g_out_b, w_out, ffn2_pre_g, ffn2_post_g, ffn2_w_gu, ffn2_w_down):
    bsz, seq, d = x.shape
    depth = w_ada.shape[0]
    assert seq % ATT_QB == 0 and seq % min(PROJ_TM, seq) == 0
    x2 = x.reshape(bsz * seq, d)
    row = lambda v: v.reshape(1, -1)
    for l in range(depth):
        mod = _ada(c, w_ada[l], b_ada[l]).reshape(bsz, N_MOD, 1, d)
        sh1, sc1, gt1, sh2, sc2, gt2, sh3, sc3, gt3 = [mod[:, i] for i in range(N_MOD)]

        x2 = _ffn(x2, seq, sh1, sc1, gt1, row(ffn1_pre_g[l]), row(ffn1_post_g[l]),
                  *_ffn_weights(ffn1_w_gu[l], ffn1_w_down[l]))

        slabs = _proj(x2, seq, sh2, sc2, row(mix_pre_g[l]), row(ln_v_g[l]), row(ln_v_b[l]),
                      w_in[l].astype(BF16))
        out_a = _attn(slabs, _rel_rows(rel_bias[l]), bsz, seq)
        out_b = _gate(slabs, w_s[l], b_s[l])
        x2 = _out_proj(x2, seq, out_a, out_b, row(g_out_a[l]), row(g_out_b[l]), gt2,
                       row(mix_post_g[l]), w_out[l].astype(BF16))

        x2 = _ffn(x2, seq, sh3, sc3, gt3, row(ffn2_pre_g[l]), row(ffn2_post_g[l]),
                  *_ffn_weights(ffn2_w_gu[l], ffn2_w_down[l]))
    return x2.reshape(bsz, seq, d)
```

```python
import functools
import math

import jax
import jax.numpy as jnp
from jax import lax
from jax.experimental import pallas as pl
from jax.experimental.pallas import tpu as pltpu

F32 = jnp.float32
BF16 = jnp.bfloat16

EPS = 1e-6
CHUNK = 64
LEFT_CHUNKS = 8
HEAD_DIM = 64
MAX_REL = 128
GATE_BLOCK = 128
N_MOD = 9

LANES = 128
BF16_SUBLANES = 16
PACK_ROWS = 64
VMEM_LIMIT = 56 * 1024 * 1024

FFN_TM = 512
FFN_TF = 1024
PROJ_TM = 1024
ATT_QB = 512
ATT_SUB = 128
ATT_HEADS = 2
ATT_KW = ATT_SUB + LEFT_CHUNKS * CHUNK
REL_ROW = ATT_KW + ATT_SUB
GATE_TM = 512
OUT_TM = 512
NEG = -1e30
LOG2E = math.log2(math.e)


def _params(sem, vmem=VMEM_LIMIT):
    return pltpu.CompilerParams(dimension_semantics=sem, vmem_limit_bytes=vmem)


def _rms(v):
    return v * lax.rsqrt(jnp.mean(v * v, axis=-1, keepdims=True) + EPS)


def _modulated(x, gain, scale, shift):
    return _rms(x) * (gain * (1.0 + scale)) + shift


def _gelu(v):
    return 0.5 * v * (1.0 + lax.erf(v * (1.0 / math.sqrt(2.0))))


def _ada_kernel(c_ref, w_ref, b_ref, o_ref):
    c = c_ref[...]
    ca = (c * jax.nn.sigmoid(c)).astype(BF16)
    o_ref[...] = jnp.dot(ca, w_ref[...].astype(BF16), preferred_element_type=F32) + b_ref[...]


def _ada(c, w, b):
    bsz, d = c.shape
    n = w.shape[1]
    rows = 8
    cp = jnp.zeros((rows, d), F32).at[:bsz].set(c)
    tn = 1024
    out = pl.pallas_call(
        _ada_kernel,
        grid=(n // tn,),
        in_specs=[pl.BlockSpec((rows, d), lambda j: (0, 0)),
                  pl.BlockSpec((d, tn), lambda j: (0, j)),
                  pl.BlockSpec((1, tn), lambda j: (0, j))],
        out_specs=pl.BlockSpec((rows, tn), lambda j: (0, j)),
        out_shape=jax.ShapeDtypeStruct((rows, n), F32),
        compiler_params=_params(("arbitrary",)),
        name="adaln",
    )(cp, w, b.reshape(1, n))
    return out[:bsz]


def _pack_gu_rows(src_ref, dst_ref):
    n_f, _, two_tf = dst_ref.shape
    tf = two_tf // 2
    d_ff = src_ref.shape[1] // 2
    for f in range(n_f):
        w = min(tf, d_ff - f * tf)
        dst_ref[f, :, :w] = src_ref[:, f * tf:f * tf + w].astype(BF16)
        dst_ref[f, :, w:2 * w] = src_ref[:, d_ff + f * tf:d_ff + f * tf + w].astype(BF16)
        if w < tf:
            dst_ref[f, :, 2 * w:] = jnp.zeros((dst_ref.shape[1], two_tf - 2 * w), BF16)


def _pack_gu_kernel(src_ref, dst_ref):
    _pack_gu_rows(src_ref, dst_ref)


def _pack_gu(w_gu, tf):
    d, two_f = w_gu.shape
    n_f = pl.cdiv(two_f // 2, tf)
    rb = PACK_ROWS
    return pl.pallas_call(
        _pack_gu_kernel,
        grid=(d // rb,),
        in_specs=[pl.BlockSpec((rb, two_f), lambda r: (r, 0))],
        out_specs=pl.BlockSpec((n_f, rb, 2 * tf), lambda r: (0, r, 0)),
        out_shape=jax.ShapeDtypeStruct((n_f, d, 2 * tf), BF16),
        compiler_params=_params(("parallel",)),
        name="pack_gu",
    )(w_gu)


def _ffn_kernel(*refs, last_width, job_kinds):
    n_job = len(job_kinds)
    x_ref, sh_ref, sc_ref, gt_ref, pre_ref, post_ref, wgu_ref, wd_ref = refs[:8]
    job_src = refs[8:8 + n_job]
    o_ref = refs[8 + n_job]
    job_dst = refs[9 + n_job:9 + 2 * n_job]
    h_ref, acc_ref = refs[9 + 2 * n_job:]
    step = pl.program_id(1)
    n_f = pl.num_programs(1)
    tf = wd_ref.shape[0]
    is_partial = _ffn_tile(pl.program_id(0), step, n_f) == n_f - 1

    def side_jobs():
        for kind, src, dst in zip(job_kinds, job_src, job_dst):
            if kind == "gu":
                _pack_gu_rows(src, dst)
            else:
                dst[...] = src[...].astype(BF16)

    def down_input(h, width):
        gu = jnp.dot(h, wgu_ref[0, :, :2 * width], preferred_element_type=F32)
        g, u = gu[:, :width], gu[:, width:]
        return (g * jax.nn.sigmoid(g) * u).astype(BF16)

    def accumulate(width):
        a = down_input(h_ref[...], width)
        acc_ref[...] += jnp.dot(a, wd_ref[:width, :], preferred_element_type=F32)

    @pl.when(step == 0)
    def _():
        h = _modulated(x_ref[...], pre_ref[...], sc_ref[0], sh_ref[0]).astype(BF16)
        h_ref[...] = h
        acc_ref[...] = jnp.dot(down_input(h, tf), wd_ref[...], preferred_element_type=F32)
        side_jobs()

    @pl.when((step > 0) & (step < n_f - 1) & jnp.logical_not(is_partial))
    def _():
        accumulate(tf)
        side_jobs()

    @pl.when(is_partial)
    def _():
        accumulate(last_width)
        side_jobs()

    @pl.when(step == n_f - 1)
    def _():
        a = down_input(h_ref[...], tf)
        half = a.shape[0] // 2
        for r in (slice(0, half), slice(half, 2 * half)):
            y = acc_ref[r, :] + jnp.dot(a[r], wd_ref[...], preferred_element_type=F32)
            o_ref[r, :] = x_ref[r, :] + (0.5 * gt_ref[0] * post_ref[...]) * _rms(y)
        side_jobs()


def _ffn_tile(i, step, n_f):
    mid = (n_f - 1) // 2
    pos = jnp.where(i % 2 == 0, step, n_f - 1 - step)
    return jnp.where(pos == mid, n_f - 1, jnp.where(pos < mid, pos, pos - 1))


def _job_rows(n_rows, n_steps):
    rb = BF16_SUBLANES
    while n_rows % rb or n_rows // rb > n_steps:
        rb += BF16_SUBLANES
    return rb


def _ffn(x2, seq, sh, sc, gt, pre_g, post_g, wgu, wd, jobs=()):
    t, d = x2.shape
    d_ff = wd.shape[0]
    tm, tf = min(FFN_TM, seq), wgu.shape[2] // 2
    n_f = wgu.shape[0]
    assert n_f >= 3 and n_f == pl.cdiv(d_ff, tf)
    n_steps = (t // tm) * n_f
    per_b = seq // tm
    mod_spec = pl.BlockSpec((1, 1, d), lambda i, f: (i // per_b, 0, 0))
    vec_spec = pl.BlockSpec((1, d), lambda i, f: (0, 0))

    job_in, job_out, job_shape = [], [], []
    for kind, w in jobs:
        rows, cols = w.shape
        rb = _job_rows(rows, n_steps)
        blk = lambda i, f, n=rows // rb: jnp.minimum(i * n_f + f, n - 1)
        job_in.append(pl.BlockSpec((rb, cols), lambda i, f, blk=blk: (blk(i, f), 0)))
        if kind == "gu":
            job_out.append(pl.BlockSpec((n_f, rb, 2 * tf), lambda i, f, blk=blk: (0, blk(i, f), 0)))
            job_shape.append(jax.ShapeDtypeStruct((n_f, rows, 2 * tf), BF16))
        else:
            job_out.append(pl.BlockSpec((rb, cols), lambda i, f, blk=blk: (blk(i, f), 0)))
            job_shape.append(jax.ShapeDtypeStruct((rows, cols), BF16))

    out = pl.pallas_call(
        functools.partial(_ffn_kernel, last_width=d_ff - (n_f - 1) * tf,
                          job_kinds=tuple(kind for kind, _ in jobs)),
        grid=(t // tm, n_f),
        in_specs=[pl.BlockSpec((tm, d), lambda i, f: (i, 0)),
                  mod_spec, mod_spec, mod_spec, vec_spec, vec_spec,
                  pl.BlockSpec((1, d, 2 * tf), lambda i, f: (_ffn_tile(i, f, n_f), 0, 0)),
                  pl.BlockSpec((tf, d), lambda i, f: (_ffn_tile(i, f, n_f), 0))] + job_in,
        out_specs=[pl.BlockSpec((tm, d), lambda i, f: (i, 0))] + job_out,
        out_shape=[jax.ShapeDtypeStruct((t, d), F32)] + job_shape,
        scratch_shapes=[pltpu.VMEM((tm, d), BF16), pltpu.VMEM((tm, d), F32)],
        compiler_params=_params(("arbitrary", "arbitrary")),
        name="ffn_convert" if jobs else "ffn",
    )(x2, sh, sc, gt, pre_g, post_g, wgu, wd, *[w for _, w in jobs])
    return out[0], out[1:]


def _proj_kernel(x_ref, sh_ref, sc_ref, pre_ref, lng_ref, lnb_ref, w_ref, o_ref, h_ref):
    j = pl.program_id(1)
    n_slab = o_ref.shape[0]
    tm = x_ref.shape[0]

    def normed(r):
        h = _modulated(x_ref[r, :], pre_ref[...], sc_ref[0], sh_ref[0]).astype(BF16)
        h_ref[r, :] = h
        return h

    def emit(epilogue, lhs=lambda r: h_ref[r, :]):
        for r in (slice(0, tm // 2), slice(tm // 2, tm)):
            val = epilogue(jnp.dot(lhs(r), w_ref[...], preferred_element_type=F32)).astype(BF16)
            for p in range(n_slab):
                o_ref[p, r, :] = val[:, p * LANES:(p + 1) * LANES]

    def gelu_layer_norm(v):
        ge = _gelu(v)
        xc = ge - jnp.mean(ge, axis=-1, keepdims=True)
        y = xc * lax.rsqrt(jnp.mean(xc * xc, axis=-1, keepdims=True) + EPS)
        return y * lng_ref[...] + lnb_ref[...]

    @pl.when(j == 0)
    def _():
        emit(lambda v: v * (LOG2E / math.sqrt(HEAD_DIM)), lhs=normed)

    @pl.when((j == 1) | (j == 2))
    def _():
        emit(lambda v: v)

    @pl.when(j == 3)
    def _():
        emit(_gelu)

    @pl.when(j == 4)
    def _():
        emit(gelu_layer_norm)


def _proj(x2, seq, sh, sc, pre_g, ln_g, ln_b, w_in):
    t, d = x2.shape
    n = w_in.shape[1]
    tn = n // 5
    n_slab = tn // LANES
    tm = min(PROJ_TM, seq)
    per_b = seq // tm
    mod_spec = pl.BlockSpec((1, 1, d), lambda i, j: (i // per_b, 0, 0))
    vec_spec = pl.BlockSpec((1, d), lambda i, j: (0, 0))
    ln_spec = pl.BlockSpec((1, tn), lambda i, j: (0, 0))
    return pl.pallas_call(
        _proj_kernel,
        grid=(t // tm, 5),
        in_specs=[pl.BlockSpec((tm, d), lambda i, j: (i, 0)),
                  mod_spec, mod_spec, vec_spec, ln_spec, ln_spec,
                  pl.BlockSpec((d, tn), lambda i, j: (0, j))],
        out_specs=pl.BlockSpec((n_slab, tm, LANES), lambda i, j: (j, i, 0)),
        out_shape=jax.ShapeDtypeStruct((5 * n_slab, t, LANES), BF16),
        scratch_shapes=[pltpu.VMEM((tm, d), BF16)],
        compiler_params=_params(("parallel", "arbitrary")),
        name="in_proj",
    )(x2, sh, sc, pre_g, ln_g, ln_b, w_in)


def _attn_kernel(q_ref, kp_ref, kc_ref, vp_ref, vc_ref, rel_ref, o_ref, tab_ref):
    n_sub = ATT_QB // ATT_SUB
    n_slab = q_ref.shape[0]
    n_head = tab_ref.shape[1]
    width = n_slab * LANES

    @pl.when((pl.program_id(1) == 0) & (pl.program_id(2) == 0))
    def _():
        r = lax.broadcasted_iota(jnp.int32, (ATT_SUB, ATT_KW), 0)
        c = lax.broadcasted_iota(jnp.int32, (ATT_SUB, ATT_KW), 1)
        lo = (r // CHUNK) * CHUNK
        in_band = (c >= lo) & (c < lo + (LEFT_CHUNKS + 1) * CHUNK)
        for h in range(n_head):
            base = jnp.broadcast_to(rel_ref[0, h:h + 1, :], (ATT_SUB, REL_ROW))
            toe = pltpu.roll(base, 0, 1, stride=1, stride_axis=0)[:, :ATT_KW] * LOG2E
            t0 = jnp.where(in_band, toe, NEG)
            tab_ref[0, h] = t0
            for i in range(n_sub):
                tab_ref[1 + i, h] = jnp.where(c + i * ATT_SUB >= ATT_QB, t0, NEG)

    qblk = pl.program_id(2)

    def lanes_of(ref, rows=slice(None)):
        return jnp.concatenate([ref[p, rows, :] for p in range(n_slab)], axis=1)

    q = lanes_of(q_ref)
    k = jnp.concatenate([lanes_of(kp_ref), lanes_of(kc_ref)], axis=0)
    v = jnp.concatenate([lanes_of(vp_ref), lanes_of(vc_ref)], axis=0)
    head_of_lane = lax.broadcasted_iota(jnp.int32, (ATT_SUB, width), 1) // HEAD_DIM

    def scores(i):
        qs = q[i * ATT_SUB:(i + 1) * ATT_SUB].astype(F32)
        qh = jnp.concatenate([jnp.where(head_of_lane == h, qs, 0.0) for h in range(n_head)], axis=0)
        ks = k[i * ATT_SUB:i * ATT_SUB + ATT_KW]
        return lax.dot_general(qh.astype(BF16), ks, (((1,), (1,)), ((), ())), preferred_element_type=F32)

    ahead = 2
    pending = [scores(i) for i in range(ahead)]
    for i in range(n_sub):
        r0 = i * ATT_SUB
        vs = v[r0:r0 + ATT_KW]
        s = pending.pop(0)
        if i + ahead < n_sub:
            pending.append(scores(i + ahead))
        s = s.reshape(n_head, ATT_SUB, ATT_KW) + tab_ref[jnp.where(qblk == 0, 1 + i, 0)]
        m = jnp.max(s, axis=-1, keepdims=True)
        p = jnp.exp2(s - m)
        l = jnp.sum(p, axis=-1, keepdims=True)
        pv = jnp.dot(p.reshape(n_head * ATT_SUB, ATT_KW).astype(BF16), vs, preferred_element_type=F32)
        pv = pv.reshape(n_head, ATT_SUB, width) / l
        out = pv[0]
        for h in range(1, n_head):
            out = jnp.where(head_of_lane == h, pv[h], out)
        for p_ in range(n_slab):
            o_ref[p_, r0:r0 + ATT_SUB, :] = out[:, p_ * LANES:(p_ + 1) * LANES].astype(o_ref.dtype)


def _attn(slabs, rel_rows, bsz, seq):
    n_grp, n_head = rel_rows.shape[:2]
    n_slab = n_head * HEAD_DIM // LANES
    t = slabs.shape[1]
    nq = seq // ATT_QB
    blk = (n_slab, ATT_QB, LANES)

    def cur(section):
        return lambda p, b, i: (section * n_grp + p, b * nq + i, 0)

    def prev(section):
        return lambda p, b, i: (section * n_grp + p, b * nq + jnp.maximum(i - 1, 0), 0)

    return pl.pallas_call(
        _attn_kernel,
        grid=(n_grp, bsz, nq),
        in_specs=[pl.BlockSpec(blk, cur(0)),
                  pl.BlockSpec(blk, prev(1)), pl.BlockSpec(blk, cur(1)),
                  pl.BlockSpec(blk, prev(2)), pl.BlockSpec(blk, cur(2)),
                  pl.BlockSpec((1, n_head, REL_ROW), lambda p, b, i: (p, 0, 0))],
        out_specs=pl.BlockSpec(blk, lambda p, b, i: (p, b * nq + i, 0)),
        out_shape=jax.ShapeDtypeStruct((n_grp * n_slab, t, LANES), BF16),
        scratch_shapes=[pltpu.VMEM((1 + ATT_QB // ATT_SUB, n_head, ATT_SUB, ATT_KW), F32)],
        compiler_params=_params(("arbitrary", "arbitrary", "arbitrary")),
        name="chunk_attn",
    )(slabs, slabs, slabs, slabs, slabs, rel_rows)


def _rel_rows(rel_bias):
    n_head = rel_bias.shape[0]
    rb = rel_bias.astype(F32)
    far = rb[:, 2 * MAX_REL:]
    n_far = ATT_KW - 2 * MAX_REL
    row = jnp.concatenate([jnp.broadcast_to(far, (n_head, n_far)), rb[:, :0:-1],
                           jnp.broadcast_to(far, (n_head, REL_ROW - ATT_KW))], axis=1)
    return row.reshape(n_head // ATT_HEADS, ATT_HEADS, REL_ROW)


def _gate_kernel(u_ref, v_ref, w_ref, b_ref, o_ref):
    n_grp = u_ref.shape[0]
    tm = u_ref.shape[1]
    ti = lax.broadcasted_iota(jnp.int32, (GATE_BLOCK, GATE_BLOCK), 0)
    si = lax.broadcasted_iota(jnp.int32, (GATE_BLOCK, GATE_BLOCK), 1)
    causal = (ti // CHUNK) >= (si // CHUNK)
    for g in range(n_grp):
        w = jnp.where(causal, w_ref[g], 0.0).astype(BF16)
        for n in range(tm // GATE_BLOCK):
            rows = slice(n * GATE_BLOCK, (n + 1) * GATE_BLOCK)
            z = jnp.dot(w, v_ref[g, rows, :], preferred_element_type=F32) + b_ref[g]
            o_ref[g, rows, :] = (u_ref[g, rows, :].astype(F32) * z).astype(o_ref.dtype)


def _gate(slabs, w_s, b_s):
    n_grp = w_s.shape[0]
    t = slabs.shape[1]
    tm = GATE_TM
    b_bc = jnp.broadcast_to(b_s.astype(F32)[:, :, None], (n_grp, GATE_BLOCK, LANES))
    return pl.pallas_call(
        _gate_kernel,
        grid=(t // tm,),
        in_specs=[pl.BlockSpec((n_grp, tm, LANES), lambda i: (3, i, 0)),
                  pl.BlockSpec((n_grp, tm, LANES), lambda i: (4, i, 0)),
                  pl.BlockSpec((n_grp, GATE_BLOCK, GATE_BLOCK), lambda i: (0, 0, 0)),
                  pl.BlockSpec((n_grp, GATE_BLOCK, LANES), lambda i: (0, 0, 0))],
        out_specs=pl.BlockSpec((n_grp, tm, LANES), lambda i: (0, i, 0)),
        out_shape=jax.ShapeDtypeStruct((n_grp, t, LANES), BF16),
        compiler_params=_params(("parallel",)),
        name="spatial_gate",
    )(slabs, slabs, w_s, b_bc)


def _out_kernel(x_ref, a_ref, b_ref, ga_ref, gb_ref, gt_ref, post_ref, w_ref, o_ref):
    n_slab = a_ref.shape[0]
    width = n_slab * LANES
    tm = x_ref.shape[0]

    def normed_slabs(src_ref, g_ref, r):
        ss = None
        for p in range(n_slab):
            v = src_ref[p, r, :].astype(F32)
            part = jnp.sum(v * v, axis=-1, keepdims=True)
            ss = part if ss is None else ss + part
        inv = lax.rsqrt(ss * (1.0 / width) + EPS)
        return [(src_ref[p, r, :].astype(F32) * inv * g_ref[:, p * LANES:(p + 1) * LANES]).astype(BF16)
                for p in range(n_slab)]

    coef = gt_ref[0] * post_ref[...]
    for r in (slice(0, tm // 2), slice(tm // 2, tm)):
        merged = jnp.concatenate(normed_slabs(a_ref, ga_ref, r) + normed_slabs(b_ref, gb_ref, r), axis=1)
        y = jnp.dot(merged, w_ref[...], preferred_element_type=F32)
        o_ref[r, :] = x_ref[r, :] + coef * _rms(y)


def _out_proj(x2, seq, a_slabs, b_slabs, g_a, g_b, gt, post_g, w_out):
    t, d = x2.shape
    n_slab = a_slabs.shape[0]
    tm = min(OUT_TM, seq)
    per_b = seq // tm
    slab_spec = pl.BlockSpec((n_slab, tm, LANES), lambda i: (0, i, 0))
    half_spec = pl.BlockSpec((1, n_slab * LANES), lambda i: (0, 0))
    return pl.pallas_call(
        _out_kernel,
        grid=(t // tm,),
        in_specs=[pl.BlockSpec((tm, d), lambda i: (i, 0)),
                  slab_spec, slab_spec, half_spec, half_spec,
                  pl.BlockSpec((1, 1, d), lambda i: (i // per_b, 0, 0)),
                  pl.BlockSpec((1, d), lambda i: (0, 0)),
                  pl.BlockSpec((d, d), lambda i: (0, 0))],
        out_specs=pl.BlockSpec((tm, d), lambda i: (i, 0)),
        out_shape=jax.ShapeDtypeStruct((t, d), F32),
        compiler_params=_params(("parallel",)),
        name="out_proj",
    )(x2, a_slabs, b_slabs, g_a, g_b, gt, post_g, w_out)


def kernel(x, c, w_ada, b_ada, ffn1_pre_g, ffn1_post_g, ffn1_w_gu, ffn1_w_down, mix_pre_g, mix_post_g, w_in, rel_bias, ln_v_g, ln_v_b, w_s, b_s, g_out_a, g_out_b, w_out, ffn2_pre_g, ffn2_post_g, ffn2_w_gu, ffn2_w_down):
    bsz, seq, d = x.shape
    depth = w_ada.shape[0]
    assert seq % ATT_QB == 0 and seq % min(PROJ_TM, seq) == 0
    x2 = x.reshape(bsz * seq, d)
    row = lambda v: v.reshape(1, -1)
    for l in range(depth):
        mod = _ada(c, w_ada[l], b_ada[l]).reshape(bsz, N_MOD, 1, d)
        sh1, sc1, gt1, sh2, sc2, gt2, sh3, sc3, gt3 = [mod[:, i] for i in range(N_MOD)]

        jobs = (("gu", ffn2_w_gu[l]), ("cast", ffn2_w_down[l]), ("cast", w_in[l]), ("cast", w_out[l]))
        x2, (wgu2, wd2, w_in_bf, w_out_bf) = _ffn(
            x2, seq, sh1, sc1, gt1, row(ffn1_pre_g[l]), row(ffn1_post_g[l]),
            _pack_gu(ffn1_w_gu[l], FFN_TF), ffn1_w_down[l].astype(BF16), jobs)

        slabs = _proj(x2, seq, sh2, sc2, row(mix_pre_g[l]), row(ln_v_g[l]), row(ln_v_b[l]), w_in_bf)
        out_a = _attn(slabs, _rel_rows(rel_bias[l]), bsz, seq)
        out_b = _gate(slabs, w_s[l], b_s[l])
        x2 = _out_proj(x2, seq, out_a, out_b, row(g_out_a[l]), row(g_out_b[l]), gt2,
                       row(mix_post_g[l]), w_out_bf)

        x2, _ = _ffn(x2, seq, sh3, sc3, gt3, row(ffn2_pre_g[l]), row(ffn2_post_g[l]), wgu2, wd2)
    return x2.reshape(bsz, seq, d)
```

```python
import functools
import math

import jax
import jax.numpy as jnp
from jax import lax
from jax.experimental import pallas as pl
from jax.experimental.pallas import tpu as pltpu

F32 = jnp.float32
BF16 = jnp.bfloat16

EPS = 1e-6
CHUNK = 64
LEFT_CHUNKS = 8
HEAD_DIM = 64
MAX_REL = 128
GATE_BLOCK = 128
N_MOD = 9

LANES = 128
BF16_SUBLANES = 16
PACK_ROWS = 64
VMEM_LIMIT = 58 * 1024 * 1024

FFN_TM = 1024
FFN_TF = 512
PROJ_TM = 1024
ATT_QB = 512
ATT_SUB = 128
ATT_HEADS = 2
ATT_KW = ATT_SUB + LEFT_CHUNKS * CHUNK
REL_ROW = ATT_KW + ATT_SUB
GATE_TM = 512
OUT_TM = 512
NEG = -1e30
LOG2E = math.log2(math.e)


def _params(sem, vmem=VMEM_LIMIT):
    return pltpu.CompilerParams(dimension_semantics=sem, vmem_limit_bytes=vmem)


def _rms(v):
    return v * lax.rsqrt(jnp.mean(v * v, axis=-1, keepdims=True) + EPS)


def _modulated(x, gain, scale, shift):
    return _rms(x) * (gain * (1.0 + scale)) + shift


def _gelu(v):
    return 0.5 * v * (1.0 + lax.erf(v * (1.0 / math.sqrt(2.0))))


def _ada_kernel(c_ref, w_ref, b_ref, o_ref):
    c = c_ref[...]
    ca = (c * jax.nn.sigmoid(c)).astype(BF16)
    o_ref[...] = jnp.dot(ca, w_ref[...].astype(BF16), preferred_element_type=F32) + b_ref[...]


def _ada(c, w, b):
    bsz, d = c.shape
    n = w.shape[1]
    rows = 8
    cp = jnp.zeros((rows, d), F32).at[:bsz].set(c)
    tn = 1024
    out = pl.pallas_call(
        _ada_kernel,
        grid=(n // tn,),
        in_specs=[pl.BlockSpec((rows, d), lambda j: (0, 0)),
                  pl.BlockSpec((d, tn), lambda j: (0, j)),
                  pl.BlockSpec((1, tn), lambda j: (0, j))],
        out_specs=pl.BlockSpec((rows, tn), lambda j: (0, j)),
        out_shape=jax.ShapeDtypeStruct((rows, n), F32),
        compiler_params=_params(("arbitrary",)),
        name="adaln",
    )(cp, w, b.reshape(1, n))
    return out[:bsz]


def _pack_gu_rows(src_ref, dst_ref):
    n_f, _, two_tf = dst_ref.shape
    tf = two_tf // 2
    d_ff = src_ref.shape[1] // 2
    for f in range(n_f):
        w = min(tf, d_ff - f * tf)
        dst_ref[f, :, :w] = src_ref[:, f * tf:f * tf + w].astype(BF16)
        dst_ref[f, :, w:2 * w] = src_ref[:, d_ff + f * tf:d_ff + f * tf + w].astype(BF16)
        if w < tf:
            dst_ref[f, :, 2 * w:] = jnp.zeros((dst_ref.shape[1], two_tf - 2 * w), BF16)


def _pack_gu_kernel(src_ref, dst_ref):
    _pack_gu_rows(src_ref, dst_ref)


def _pack_gu(w_gu, tf):
    d, two_f = w_gu.shape
    n_f = pl.cdiv(two_f // 2, tf)
    rb = PACK_ROWS
    return pl.pallas_call(
        _pack_gu_kernel,
        grid=(d // rb,),
        in_specs=[pl.BlockSpec((rb, two_f), lambda r: (r, 0))],
        out_specs=pl.BlockSpec((n_f, rb, 2 * tf), lambda r: (0, r, 0)),
        out_shape=jax.ShapeDtypeStruct((n_f, d, 2 * tf), BF16),
        compiler_params=_params(("parallel",)),
        name="pack_gu",
    )(w_gu)


def _ffn_kernel(*refs, last_width, job_kinds):
    n_job = len(job_kinds)
    x_ref, sh_ref, sc_ref, gt_ref, pre_ref, post_ref, wgu_ref, wd_ref = refs[:8]
    job_src = refs[8:8 + n_job]
    o_ref = refs[8 + n_job]
    job_dst = refs[9 + n_job:9 + 2 * n_job]
    (h_ref,) = refs[9 + 2 * n_job:]
    acc_ref = o_ref
    step = pl.program_id(1)
    n_f = pl.num_programs(1)
    tf = wd_ref.shape[0]
    is_partial = _ffn_tile(pl.program_id(0), step, n_f) == n_f - 1

    def side_jobs():
        for kind, src, dst in zip(job_kinds, job_src, job_dst):
            if kind == "gu":
                _pack_gu_rows(src, dst)
            else:
                dst[...] = src[...].astype(BF16)

    def down_input(h, width):
        gu = jnp.dot(h, wgu_ref[0, :, :2 * width], preferred_element_type=F32)
        g, u = gu[:, :width], gu[:, width:]
        return (g * jax.nn.sigmoid(g) * u).astype(BF16)

    tm = x_ref.shape[0]
    halves = (slice(0, tm // 2), slice(tm // 2, tm))

    def accumulate(width):
        for r in halves:
            a = down_input(h_ref[r, :], width)
            acc_ref[r, :] += jnp.dot(a, wd_ref[:width, :], preferred_element_type=F32)

    @pl.when(step == 0)
    def _():
        for r in halves:
            h = _modulated(x_ref[r, :], pre_ref[...], sc_ref[0], sh_ref[0]).astype(BF16)
            h_ref[r, :] = h
            acc_ref[r, :] = jnp.dot(down_input(h, tf), wd_ref[...], preferred_element_type=F32)
        side_jobs()

    @pl.when((step > 0) & (step < n_f - 1) & jnp.logical_not(is_partial))
    def _():
        accumulate(tf)
        side_jobs()

    @pl.when(is_partial)
    def _():
        accumulate(last_width)
        side_jobs()

    @pl.when(step == n_f - 1)
    def _():
        for r in halves:
            a = down_input(h_ref[r, :], tf)
            y = acc_ref[r, :] + jnp.dot(a, wd_ref[...], preferred_element_type=F32)
            o_ref[r, :] = x_ref[r, :] + (0.5 * gt_ref[0] * post_ref[...]) * _rms(y)
        side_jobs()


def _ffn_tile(i, step, n_f):
    mid = (n_f - 1) // 2
    pos = jnp.where(i % 2 == 0, step, n_f - 1 - step)
    return jnp.where(pos == mid, n_f - 1, jnp.where(pos < mid, pos, pos - 1))


def _job_rows(n_rows, n_steps):
    rb = BF16_SUBLANES
    while n_rows % rb or n_rows // rb > n_steps:
        rb += BF16_SUBLANES
    return rb


def _ffn(x2, seq, sh, sc, gt, pre_g, post_g, wgu, wd, jobs=()):
    t, d = x2.shape
    d_ff = wd.shape[0]
    tm, tf = min(FFN_TM, seq), wgu.shape[2] // 2
    n_f = wgu.shape[0]
    assert n_f >= 3 and n_f == pl.cdiv(d_ff, tf)
    n_steps = (t // tm) * n_f
    per_b = seq // tm
    mod_spec = pl.BlockSpec((1, 1, d), lambda i, f: (i // per_b, 0, 0))
    vec_spec = pl.BlockSpec((1, d), lambda i, f: (0, 0))

    job_in, job_out, job_shape = [], [], []
    for kind, w in jobs:
        rows, cols = w.shape
        rb = _job_rows(rows, n_steps)
        blk = lambda i, f, n=rows // rb: jnp.minimum(i * n_f + f, n - 1)
        job_in.append(pl.BlockSpec((rb, cols), lambda i, f, blk=blk: (blk(i, f), 0)))
        if kind == "gu":
            job_out.append(pl.BlockSpec((n_f, rb, 2 * tf), lambda i, f, blk=blk: (0, blk(i, f), 0)))
            job_shape.append(jax.ShapeDtypeStruct((n_f, rows, 2 * tf), BF16))
        else:
            job_out.append(pl.BlockSpec((rb, cols), lambda i, f, blk=blk: (blk(i, f), 0)))
            job_shape.append(jax.ShapeDtypeStruct((rows, cols), BF16))

    out = pl.pallas_call(
        functools.partial(_ffn_kernel, last_width=d_ff - (n_f - 1) * tf,
                          job_kinds=tuple(kind for kind, _ in jobs)),
        grid=(t // tm, n_f),
        in_specs=[pl.BlockSpec((tm, d), lambda i, f: (i, 0)),
                  mod_spec, mod_spec, mod_spec, vec_spec, vec_spec,
                  pl.BlockSpec((1, d, 2 * tf), lambda i, f: (_ffn_tile(i, f, n_f), 0, 0)),
                  pl.BlockSpec((tf, d), lambda i, f: (_ffn_tile(i, f, n_f), 0))] + job_in,
        out_specs=[pl.BlockSpec((tm, d), lambda i, f: (i, 0))] + job_out,
        out_shape=[jax.ShapeDtypeStruct((t, d), F32)] + job_shape,
        scratch_shapes=[pltpu.VMEM((tm, d), BF16)],
        compiler_params=_params(("arbitrary", "arbitrary")),
        name="ffn_convert" if jobs else "ffn",
    )(x2, sh, sc, gt, pre_g, post_g, wgu, wd, *[w for _, w in jobs])
    return out[0], out[1:]


def _proj_kernel(x_ref, sh_ref, sc_ref, pre_ref, lng_ref, lnb_ref, w_ref, o_ref, h_ref):
    j = pl.program_id(1)
    n_slab = o_ref.shape[0]
    tm = x_ref.shape[0]

    def normed(r):
        h = _modulated(x_ref[r, :], pre_ref[...], sc_ref[0], sh_ref[0]).astype(BF16)
        h_ref[r, :] = h
        return h

    def emit(epilogue, lhs=lambda r: h_ref[r, :]):
        for r in (slice(0, tm // 2), slice(tm // 2, tm)):
            val = epilogue(jnp.dot(lhs(r), w_ref[...], preferred_element_type=F32)).astype(BF16)
            for p in range(n_slab):
                o_ref[p, r, :] = val[:, p * LANES:(p + 1) * LANES]

    def gelu_layer_norm(v):
        ge = _gelu(v)
        xc = ge - jnp.mean(ge, axis=-1, keepdims=True)
        y = xc * lax.rsqrt(jnp.mean(xc * xc, axis=-1, keepdims=True) + EPS)
        return y * lng_ref[...] + lnb_ref[...]

    @pl.when(j == 0)
    def _():
        emit(lambda v: v * (LOG2E / math.sqrt(HEAD_DIM)), lhs=normed)

    @pl.when((j == 1) | (j == 2))
    def _():
        emit(lambda v: v)

    @pl.when(j == 3)
    def _():
        emit(_gelu)

    @pl.when(j == 4)
    def _():
        emit(gelu_layer_norm)


def _proj(x2, seq, sh, sc, pre_g, ln_g, ln_b, w_in):
    t, d = x2.shape
    n = w_in.shape[1]
    tn = n // 5
    n_slab = tn // LANES
    tm = min(PROJ_TM, seq)
    per_b = seq // tm
    mod_spec = pl.BlockSpec((1, 1, d), lambda i, j: (i // per_b, 0, 0))
    vec_spec = pl.BlockSpec((1, d), lambda i, j: (0, 0))
    ln_spec = pl.BlockSpec((1, tn), lambda i, j: (0, 0))
    return pl.pallas_call(
        _proj_kernel,
        grid=(t // tm, 5),
        in_specs=[pl.BlockSpec((tm, d), lambda i, j: (i, 0)),
                  mod_spec, mod_spec, vec_spec, ln_spec, ln_spec,
                  pl.BlockSpec((d, tn), lambda i, j: (0, j))],
        out_specs=pl.BlockSpec((n_slab, tm, LANES), lambda i, j: (j, i, 0)),
        out_shape=jax.ShapeDtypeStruct((5 * n_slab, t, LANES), BF16),
        scratch_shapes=[pltpu.VMEM((tm, d), BF16)],
        compiler_params=_params(("parallel", "arbitrary")),
        name="in_proj",
    )(x2, sh, sc, pre_g, ln_g, ln_b, w_in)


def _attn_kernel(q_ref, kp_ref, kc_ref, vp_ref, vc_ref, rel_ref, o_ref, tab_ref):
    n_sub = ATT_QB // ATT_SUB
    n_slab = q_ref.shape[0]
    n_head = tab_ref.shape[1]
    width = n_slab * LANES

    @pl.when((pl.program_id(1) == 0) & (pl.program_id(2) == 0))
    def _():
        r = lax.broadcasted_iota(jnp.int32, (ATT_SUB, ATT_KW), 0)
        c = lax.broadcasted_iota(jnp.int32, (ATT_SUB, ATT_KW), 1)
        lo = (r // CHUNK) * CHUNK
        in_band = (c >= lo) & (c < lo + (LEFT_CHUNKS + 1) * CHUNK)
        for h in range(n_head):
            base = jnp.broadcast_to(rel_ref[0, h:h + 1, :], (ATT_SUB, REL_ROW))
            toe = pltpu.roll(base, 0, 1, stride=1, stride_axis=0)[:, :ATT_KW] * LOG2E
            t0 = jnp.where(in_band, toe, NEG)
            tab_ref[0, h] = t0
            for i in range(n_sub):
                tab_ref[1 + i, h] = jnp.where(c + i * ATT_SUB >= ATT_QB, t0, NEG)

    qblk = pl.program_id(2)

    def lanes_of(ref, rows=slice(None)):
        return jnp.concatenate([ref[p, rows, :] for p in range(n_slab)], axis=1)

    q = lanes_of(q_ref)
    k = jnp.concatenate([lanes_of(kp_ref), lanes_of(kc_ref)], axis=0)
    v = jnp.concatenate([lanes_of(vp_ref), lanes_of(vc_ref)], axis=0)
    head_of_lane = lax.broadcasted_iota(jnp.int32, (ATT_SUB, width), 1) // HEAD_DIM

    def scores(i):
        qs = q[i * ATT_SUB:(i + 1) * ATT_SUB].astype(F32)
        qh = jnp.concatenate([jnp.where(head_of_lane == h, qs, 0.0) for h in range(n_head)], axis=0)
        ks = k[i * ATT_SUB:i * ATT_SUB + ATT_KW]
        return lax.dot_general(qh.astype(BF16), ks, (((1,), (1,)), ((), ())), preferred_element_type=F32)

    ahead = 2
    pending = [scores(i) for i in range(ahead)]
    for i in range(n_sub):
        r0 = i * ATT_SUB
        vs = v[r0:r0 + ATT_KW]
        s = pending.pop(0)
        if i + ahead < n_sub:
            pending.append(scores(i + ahead))
        s = s.reshape(n_head, ATT_SUB, ATT_KW) + tab_ref[jnp.where(qblk == 0, 1 + i, 0)]
        m = jnp.max(s, axis=-1, keepdims=True)
        p = jnp.exp2(s - m)
        l = jnp.sum(p, axis=-1, keepdims=True)
        pv = jnp.dot(p.reshape(n_head * ATT_SUB, ATT_KW).astype(BF16), vs, preferred_element_type=F32)
        pv = pv.reshape(n_head, ATT_SUB, width) / l
        out = pv[0]
        for h in range(1, n_head):
            out = jnp.where(head_of_lane == h, pv[h], out)
        for p_ in range(n_slab):
            o_ref[p_, r0:r0 + ATT_SUB, :] = out[:, p_ * LANES:(p_ + 1) * LANES].astype(o_ref.dtype)


def _attn(slabs, rel_rows, bsz, seq):
    n_grp, n_head = rel_rows.shape[:2]
    n_slab = n_head * HEAD_DIM // LANES
    t = slabs.shape[1]
    nq = seq // ATT_QB
    blk = (n_slab, ATT_QB, LANES)

    def cur(section):
        return lambda p, b, i: (section * n_grp + p, b * nq + i, 0)

    def prev(section):
        return lambda p, b, i: (section * n_grp + p, b * nq + jnp.maximum(i - 1, 0), 0)

    return pl.pallas_call(
        _attn_kernel,
        grid=(n_grp, bsz, nq),
        in_specs=[pl.BlockSpec(blk, cur(0)),
                  pl.BlockSpec(blk, prev(1)), pl.BlockSpec(blk, cur(1)),
                  pl.BlockSpec(blk, prev(2)), pl.BlockSpec(blk, cur(2)),
                  pl.BlockSpec((1, n_head, REL_ROW), lambda p, b, i: (p, 0, 0))],
        out_specs=pl.BlockSpec(blk, lambda p, b, i: (p, b * nq + i, 0)),
        out_shape=jax.ShapeDtypeStruct((n_grp * n_slab, t, LANES), BF16),
        scratch_shapes=[pltpu.VMEM((1 + ATT_QB // ATT_SUB, n_head, ATT_SUB, ATT_KW), F32)],
        compiler_params=_params(("arbitrary", "arbitrary", "arbitrary")),
        name="chunk_attn",
    )(slabs, slabs, slabs, slabs, slabs, rel_rows)


def _rel_rows(rel_bias):
    n_head = rel_bias.shape[0]
    rb = rel_bias.astype(F32)
    far = rb[:, 2 * MAX_REL:]
    n_far = ATT_KW - 2 * MAX_REL
    row = jnp.concatenate([jnp.broadcast_to(far, (n_head, n_far)), rb[:, :0:-1],
                           jnp.broadcast_to(far, (n_head, REL_ROW - ATT_KW))], axis=1)
    return row.reshape(n_head // ATT_HEADS, ATT_HEADS, REL_ROW)


def _gate_kernel(u_ref, v_ref, w_ref, b_ref, o_ref):
    n_grp = u_ref.shape[0]
    tm = u_ref.shape[1]
    ti = lax.broadcasted_iota(jnp.int32, (GATE_BLOCK, GATE_BLOCK), 0)
    si = lax.broadcasted_iota(jnp.int32, (GATE_BLOCK, GATE_BLOCK), 1)
    causal = (ti // CHUNK) >= (si // CHUNK)
    for g in range(n_grp):
        w = jnp.where(causal, w_ref[g], 0.0).astype(BF16)
        for n in range(tm // GATE_BLOCK):
            rows = slice(n * GATE_BLOCK, (n + 1) * GATE_BLOCK)
            z = jnp.dot(w, v_ref[g, rows, :], preferred_element_type=F32) + b_ref[g]
            o_ref[g, rows, :] = (u_ref[g, rows, :].astype(F32) * z).astype(o_ref.dtype)


def _gate(slabs, w_s, b_s):
    n_grp = w_s.shape[0]
    t = slabs.shape[1]
    tm = GATE_TM
    b_bc = jnp.broadcast_to(b_s.astype(F32)[:, :, None], (n_grp, GATE_BLOCK, LANES))
    return pl.pallas_call(
        _gate_kernel,
        grid=(t // tm,),
        in_specs=[pl.BlockSpec((n_grp, tm, LANES), lambda i: (3, i, 0)),
                  pl.BlockSpec((n_grp, tm, LANES), lambda i: (4, i, 0)),
                  pl.BlockSpec((n_grp, GATE_BLOCK, GATE_BLOCK), lambda i: (0, 0, 0)),
                  pl.BlockSpec((n_grp, GATE_BLOCK, LANES), lambda i: (0, 0, 0))],
        out_specs=pl.BlockSpec((n_grp, tm, LANES), lambda i: (0, i, 0)),
        out_shape=jax.ShapeDtypeStruct((n_grp, t, LANES), BF16),
        compiler_params=_params(("parallel",)),
        name="spatial_gate",
    )(slabs, slabs, w_s, b_bc)


def _out_kernel(x_ref, a_ref, b_ref, ga_ref, gb_ref, gt_ref, post_ref, w_ref, o_ref):
    n_slab = a_ref.shape[0]
    width = n_slab * LANES
    tm = x_ref.shape[0]

    def normed_slabs(src_ref, g_ref, r):
        ss = None
        for p in range(n_slab):
            v = src_ref[p, r, :].astype(F32)
            part = jnp.sum(v * v, axis=-1, keepdims=True)
            ss = part if ss is None else ss + part
        inv = lax.rsqrt(ss * (1.0 / width) + EPS)
        return [(src_ref[p, r, :].astype(F32) * inv * g_ref[:, p * LANES:(p + 1) * LANES]).astype(BF16)
                for p in range(n_slab)]

    coef = gt_ref[0] * post_ref[...]
    for r in (slice(0, tm // 2), slice(tm // 2, tm)):
        merged = jnp.concatenate(normed_slabs(a_ref, ga_ref, r) + normed_slabs(b_ref, gb_ref, r), axis=1)
        y = jnp.dot(merged, w_ref[...], preferred_element_type=F32)
        o_ref[r, :] = x_ref[r, :] + coef * _rms(y)


def _out_proj(x2, seq, a_slabs, b_slabs, g_a, g_b, gt, post_g, w_out):
    t, d = x2.shape
    n_slab = a_slabs.shape[0]
    tm = min(OUT_TM, seq)
    per_b = seq // tm
    slab_spec = pl.BlockSpec((n_slab, tm, LANES), lambda i: (0, i, 0))
    half_spec = pl.BlockSpec((1, n_slab * LANES), lambda i: (0, 0))
    return pl.pallas_call(
        _out_kernel,
        grid=(t // tm,),
        in_specs=[pl.BlockSpec((tm, d), lambda i: (i, 0)),
                  slab_spec, slab_spec, half_spec, half_spec,
                  pl.BlockSpec((1, 1, d), lambda i: (i // per_b, 0, 0)),
                  pl.BlockSpec((1, d), lambda i: (0, 0)),
                  pl.BlockSpec((d, d), lambda i: (0, 0))],
        out_specs=pl.BlockSpec((tm, d), lambda i: (i, 0)),
        out_shape=jax.ShapeDtypeStruct((t, d), F32),
        compiler_params=_params(("parallel",)),
        name="out_proj",
    )(x2, a_slabs, b_slabs, g_a, g_b, gt, post_g, w_out)


def kernel(x, c, w_ada, b_ada, ffn1_pre_g, ffn1_post_g, ffn1_w_gu, ffn1_w_down, mix_pre_g, mix_post_g, w_in, rel_bias, ln_v_g, ln_v_b, w_s, b_s, g_out_a, g_out_b, w_out, ffn2_pre_g, ffn2_post_g, ffn2_w_gu, ffn2_w_down):
    bsz, seq, d = x.shape
    depth = w_ada.shape[0]
    assert seq % ATT_QB == 0 and seq % min(PROJ_TM, seq) == 0
    x2 = x.reshape(bsz * seq, d)
    row = lambda v: v.reshape(1, -1)
    for l in range(depth):
        mod = _ada(c, w_ada[l], b_ada[l]).reshape(bsz, N_MOD, 1, d)
        sh1, sc1, gt1, sh2, sc2, gt2, sh3, sc3, gt3 = [mod[:, i] for i in range(N_MOD)]

        jobs = (("gu", ffn2_w_gu[l]), ("cast", ffn2_w_down[l]), ("cast", w_in[l]), ("cast", w_out[l]))
        x2, (wgu2, wd2, w_in_bf, w_out_bf) = _ffn(
            x2, seq, sh1, sc1, gt1, row(ffn1_pre_g[l]), row(ffn1_post_g[l]),
            _pack_gu(ffn1_w_gu[l], FFN_TF), ffn1_w_down[l].astype(BF16), jobs)

        slabs = _proj(x2, seq, sh2, sc2, row(mix_pre_g[l]), row(ln_v_g[l]), row(ln_v_b[l]), w_in_bf)
        out_a = _attn(slabs, _rel_rows(rel_bias[l]), bsz, seq)
        out_b = _gate(slabs, w_s[l], b_s[l])
        x2 = _out_proj(x2, seq, out_a, out_b, row(g_out_a[l]), row(g_out_b[l]), gt2,
                       row(mix_post_g[l]), w_out_bf)

        x2, _ = _ffn(x2, seq, sh3, sc3, gt3, row(ffn2_pre_g[l]), row(ffn2_post_g[l]), wgu2, wd2)
    return x2.reshape(bsz, seq, d)
```

```python
import functools
import math

import jax
import jax.numpy as jnp
from jax import lax
from jax.experimental import pallas as pl
from jax.experimental.pallas import tpu as pltpu

F32 = jnp.float32
BF16 = jnp.bfloat16

EPS = 1e-6
CHUNK = 64
LEFT_CHUNKS = 8
HEAD_DIM = 64
MAX_REL = 128
GATE_BLOCK = 128
N_MOD = 9

LANES = 128
BF16_SUBLANES = 16
PACK_ROWS = 64
VMEM_LIMIT = 58 * 1024 * 1024

FFN_TM = 1024
FFN_TF = 512
PROJ_TM = 1024
ATT_QB = 512
ATT_SUB = 128
ATT_HEADS = 2
ATT_KW = ATT_SUB + LEFT_CHUNKS * CHUNK
REL_ROW = ATT_KW + ATT_SUB
GATE_TM = 512
OUT_TM = 1024
NEG = -1e30
LOG2E = math.log2(math.e)


def _params(sem, vmem=VMEM_LIMIT):
    return pltpu.CompilerParams(dimension_semantics=sem, vmem_limit_bytes=vmem)


def _rms(v):
    return v * lax.rsqrt(jnp.mean(v * v, axis=-1, keepdims=True) + EPS)


def _modulated(x, gain, scale, shift):
    return _rms(x) * (gain * (1.0 + scale)) + shift


def _gelu(v):
    return 0.5 * v * (1.0 + lax.erf(v * (1.0 / math.sqrt(2.0))))


def _ada_kernel(c_ref, w_ref, b_ref, o_ref):
    c = c_ref[...]
    ca = (c * jax.nn.sigmoid(c)).astype(BF16)
    o_ref[...] = jnp.dot(ca, w_ref[...].astype(BF16), preferred_element_type=F32) + b_ref[...]


def _ada(c, w, b):
    bsz, d = c.shape
    n = w.shape[1]
    rows = 8
    cp = jnp.zeros((rows, d), F32).at[:bsz].set(c)
    tn = 1024
    out = pl.pallas_call(
        _ada_kernel,
        grid=(n // tn,),
        in_specs=[pl.BlockSpec((rows, d), lambda j: (0, 0)),
                  pl.BlockSpec((d, tn), lambda j: (0, j)),
                  pl.BlockSpec((1, tn), lambda j: (0, j))],
        out_specs=pl.BlockSpec((rows, tn), lambda j: (0, j)),
        out_shape=jax.ShapeDtypeStruct((rows, n), F32),
        compiler_params=_params(("arbitrary",)),
        name="adaln",
    )(cp, w, b.reshape(1, n))
    return out[:bsz]


def _pack_gu_rows(src_ref, dst_ref):
    n_f, _, two_tf = dst_ref.shape
    tf = two_tf // 2
    d_ff = src_ref.shape[1] // 2
    for f in range(n_f):
        w = min(tf, d_ff - f * tf)
        dst_ref[f, :, :w] = src_ref[:, f * tf:f * tf + w].astype(BF16)
        dst_ref[f, :, w:2 * w] = src_ref[:, d_ff + f * tf:d_ff + f * tf + w].astype(BF16)
        if w < tf:
            dst_ref[f, :, 2 * w:] = jnp.zeros((dst_ref.shape[1], two_tf - 2 * w), BF16)


def _pack_gu_kernel(src_ref, dst_ref):
    _pack_gu_rows(src_ref, dst_ref)


def _pack_gu(w_gu, tf):
    d, two_f = w_gu.shape
    n_f = pl.cdiv(two_f // 2, tf)
    rb = PACK_ROWS
    return pl.pallas_call(
        _pack_gu_kernel,
        grid=(d // rb,),
        in_specs=[pl.BlockSpec((rb, two_f), lambda r: (r, 0))],
        out_specs=pl.BlockSpec((n_f, rb, 2 * tf), lambda r: (0, r, 0)),
        out_shape=jax.ShapeDtypeStruct((n_f, d, 2 * tf), BF16),
        compiler_params=_params(("parallel",)),
        name="pack_gu",
    )(w_gu)


def _ffn_kernel(*refs, last_width, job_kinds):
    n_job = len(job_kinds)
    x_ref, sh_ref, sc_ref, gt_ref, pre_ref, post_ref, wgu_ref, wd_ref = refs[:8]
    job_src = refs[8:8 + n_job]
    o_ref = refs[8 + n_job]
    job_dst = refs[9 + n_job:9 + 2 * n_job]
    (h_ref,) = refs[9 + 2 * n_job:]
    acc_ref = o_ref
    step = pl.program_id(1)
    n_f = pl.num_programs(1)
    tf = wd_ref.shape[0]
    is_partial = _ffn_tile(pl.program_id(0), step, n_f) == n_f - 1

    def side_jobs():
        for kind, src, dst in zip(job_kinds, job_src, job_dst):
            if kind == "gu":
                _pack_gu_rows(src, dst)
            else:
                dst[...] = src[...].astype(BF16)

    def down_input(h, width):
        gu = jnp.dot(h, wgu_ref[0, :, :2 * width], preferred_element_type=F32)
        g, u = gu[:, :width], gu[:, width:]
        return (g * jax.nn.sigmoid(g) * u).astype(BF16)

    tm = x_ref.shape[0]
    halves = (slice(0, tm // 2), slice(tm // 2, tm))

    def accumulate(width):
        for r in halves:
            a = down_input(h_ref[r, :], width)
            acc_ref[r, :] += jnp.dot(a, wd_ref[:width, :], preferred_element_type=F32)

    @pl.when(step == 0)
    def _():
        for r in halves:
            h = _modulated(x_ref[r, :], pre_ref[...], sc_ref[0], sh_ref[0]).astype(BF16)
            h_ref[r, :] = h
            acc_ref[r, :] = jnp.dot(down_input(h, tf), wd_ref[...], preferred_element_type=F32)
        side_jobs()

    @pl.when((step > 0) & (step < n_f - 1) & jnp.logical_not(is_partial))
    def _():
        accumulate(tf)
        side_jobs()

    @pl.when(is_partial)
    def _():
        accumulate(last_width)
        side_jobs()

    @pl.when(step == n_f - 1)
    def _():
        for r in halves:
            a = down_input(h_ref[r, :], tf)
            y = acc_ref[r, :] + jnp.dot(a, wd_ref[...], preferred_element_type=F32)
            o_ref[r, :] = x_ref[r, :] + (0.5 * gt_ref[0] * post_ref[...]) * _rms(y)
        side_jobs()


def _ffn_tile(i, step, n_f):
    mid = (n_f - 1) // 2
    pos = jnp.where(i % 2 == 0, step, n_f - 1 - step)
    return jnp.where(pos == mid, n_f - 1, jnp.where(pos < mid, pos, pos - 1))


def _job_rows(n_rows, n_steps):
    rb = BF16_SUBLANES
    while n_rows % rb or n_rows // rb > n_steps:
        rb += BF16_SUBLANES
    return rb


def _ffn(x2, seq, sh, sc, gt, pre_g, post_g, wgu, wd, jobs=()):
    t, d = x2.shape
    d_ff = wd.shape[0]
    tm, tf = min(FFN_TM, seq), wgu.shape[2] // 2
    n_f = wgu.shape[0]
    assert n_f >= 3 and n_f == pl.cdiv(d_ff, tf)
    n_steps = (t // tm) * n_f
    per_b = seq // tm
    mod_spec = pl.BlockSpec((1, 1, d), lambda i, f: (i // per_b, 0, 0))
    vec_spec = pl.BlockSpec((1, d), lambda i, f: (0, 0))

    job_in, job_out, job_shape = [], [], []
    for kind, w in jobs:
        rows, cols = w.shape
        rb = _job_rows(rows, n_steps)
        blk = lambda i, f, n=rows // rb: jnp.minimum(i * n_f + f, n - 1)
        job_in.append(pl.BlockSpec((rb, cols), lambda i, f, blk=blk: (blk(i, f), 0)))
        if kind == "gu":
            job_out.append(pl.BlockSpec((n_f, rb, 2 * tf), lambda i, f, blk=blk: (0, blk(i, f), 0)))
            job_shape.append(jax.ShapeDtypeStruct((n_f, rows, 2 * tf), BF16))
        else:
            job_out.append(pl.BlockSpec((rb, cols), lambda i, f, blk=blk: (blk(i, f), 0)))
            job_shape.append(jax.ShapeDtypeStruct((rows, cols), BF16))

    out = pl.pallas_call(
        functools.partial(_ffn_kernel, last_width=d_ff - (n_f - 1) * tf,
                          job_kinds=tuple(kind for kind, _ in jobs)),
        grid=(t // tm, n_f),
        in_specs=[pl.BlockSpec((tm, d), lambda i, f: (i, 0)),
                  mod_spec, mod_spec, mod_spec, vec_spec, vec_spec,
                  pl.BlockSpec((1, d, 2 * tf), lambda i, f: (_ffn_tile(i, f, n_f), 0, 0)),
                  pl.BlockSpec((tf, d), lambda i, f: (_ffn_tile(i, f, n_f), 0))] + job_in,
        out_specs=[pl.BlockSpec((tm, d), lambda i, f: (i, 0))] + job_out,
        out_shape=[jax.ShapeDtypeStruct((t, d), F32)] + job_shape,
        scratch_shapes=[pltpu.VMEM((tm, d), BF16)],
        compiler_params=_params(("arbitrary", "arbitrary")),
        name="ffn_convert" if jobs else "ffn",
    )(x2, sh, sc, gt, pre_g, post_g, wgu, wd, *[w for _, w in jobs])
    return out[0], out[1:]


def _proj_kernel(x_ref, sh_ref, sc_ref, pre_ref, lng_ref, lnb_ref, w_ref, o_ref, h_ref):
    j = pl.program_id(1)
    n_slab = o_ref.shape[0]
    tm = x_ref.shape[0]

    def normed(r):
        h = _modulated(x_ref[r, :], pre_ref[...], sc_ref[0], sh_ref[0]).astype(BF16)
        h_ref[r, :] = h
        return h

    def emit(epilogue, lhs=lambda r: h_ref[r, :]):
        for r in [slice(q * tm // 4, (q + 1) * tm // 4) for q in range(4)]:
            val = epilogue(jnp.dot(lhs(r), w_ref[...], preferred_element_type=F32)).astype(BF16)
            for p in range(n_slab):
                o_ref[p, r, :] = val[:, p * LANES:(p + 1) * LANES]

    def gelu_layer_norm(v):
        ge = _gelu(v)
        xc = ge - jnp.mean(ge, axis=-1, keepdims=True)
        y = xc * lax.rsqrt(jnp.mean(xc * xc, axis=-1, keepdims=True) + EPS)
        return y * lng_ref[...] + lnb_ref[...]

    @pl.when(j == 0)
    def _():
        emit(lambda v: v * (LOG2E / math.sqrt(HEAD_DIM)), lhs=normed)

    @pl.when((j == 1) | (j == 2))
    def _():
        emit(lambda v: v)

    @pl.when(j == 3)
    def _():
        emit(_gelu)

    @pl.when(j == 4)
    def _():
        emit(gelu_layer_norm)


def _proj(x2, seq, sh, sc, pre_g, ln_g, ln_b, w_in):
    t, d = x2.shape
    n = w_in.shape[1]
    tn = n // 5
    n_slab = tn // LANES
    tm = min(PROJ_TM, seq)
    per_b = seq // tm
    mod_spec = pl.BlockSpec((1, 1, d), lambda i, j: (i // per_b, 0, 0))
    vec_spec = pl.BlockSpec((1, d), lambda i, j: (0, 0))
    ln_spec = pl.BlockSpec((1, tn), lambda i, j: (0, 0))
    return pl.pallas_call(
        _proj_kernel,
        grid=(t // tm, 5),
        in_specs=[pl.BlockSpec((tm, d), lambda i, j: (i, 0)),
                  mod_spec, mod_spec, vec_spec, ln_spec, ln_spec,
                  pl.BlockSpec((d, tn), lambda i, j: (0, j))],
        out_specs=pl.BlockSpec((n_slab, tm, LANES), lambda i, j: (j, i, 0)),
        out_shape=jax.ShapeDtypeStruct((5 * n_slab, t, LANES), BF16),
        scratch_shapes=[pltpu.VMEM((tm, d), BF16)],
        compiler_params=_params(("parallel", "arbitrary")),
        name="in_proj",
    )(x2, sh, sc, pre_g, ln_g, ln_b, w_in)


def _attn_kernel(q_ref, kp_ref, kc_ref, vp_ref, vc_ref, rel_ref, o_ref, tab_ref):
    n_sub = ATT_QB // ATT_SUB
    n_slab = q_ref.shape[0]
    n_head = tab_ref.shape[1]
    width = n_slab * LANES

    @pl.when((pl.program_id(1) == 0) & (pl.program_id(2) == 0))
    def _():
        r = lax.broadcasted_iota(jnp.int32, (ATT_SUB, ATT_KW), 0)
        c = lax.broadcasted_iota(jnp.int32, (ATT_SUB, ATT_KW), 1)
        lo = (r // CHUNK) * CHUNK
        in_band = (c >= lo) & (c < lo + (LEFT_CHUNKS + 1) * CHUNK)
        for h in range(n_head):
            base = jnp.broadcast_to(rel_ref[0, h:h + 1, :], (ATT_SUB, REL_ROW))
            toe = pltpu.roll(base, 0, 1, stride=1, stride_axis=0)[:, :ATT_KW] * LOG2E
            t0 = jnp.where(in_band, toe, NEG)
            tab_ref[0, h] = t0
            for i in range(n_sub):
                tab_ref[1 + i, h] = jnp.where(c + i * ATT_SUB >= ATT_QB, t0, NEG)

    qblk = pl.program_id(2)

    def lanes_of(ref, rows=slice(None)):
        return jnp.concatenate([ref[p, rows, :] for p in range(n_slab)], axis=1)

    q = lanes_of(q_ref)
    k = jnp.concatenate([lanes_of(kp_ref), lanes_of(kc_ref)], axis=0)
    v = jnp.concatenate([lanes_of(vp_ref), lanes_of(vc_ref)], axis=0)
    head_of_lane = lax.broadcasted_iota(jnp.int32, (ATT_SUB, width), 1) // HEAD_DIM

    def scores(i):
        qs = q[i * ATT_SUB:(i + 1) * ATT_SUB].astype(F32)
        qh = jnp.concatenate([jnp.where(head_of_lane == h, qs, 0.0) for h in range(n_head)], axis=0)
        ks = k[i * ATT_SUB:i * ATT_SUB + ATT_KW]
        return lax.dot_general(qh.astype(BF16), ks, (((1,), (1,)), ((), ())), preferred_element_type=F32)

    ahead = 2
    pending = [scores(i) for i in range(ahead)]
    for i in range(n_sub):
        r0 = i * ATT_SUB
        vs = v[r0:r0 + ATT_KW]
        s = pending.pop(0)
        if i + ahead < n_sub:
            pending.append(scores(i + ahead))
        s = s.reshape(n_head, ATT_SUB, ATT_KW) + tab_ref[jnp.where(qblk == 0, 1 + i, 0)]
        m = jnp.max(s, axis=-1, keepdims=True)
        p = jnp.exp2(s - m)
        l = jnp.sum(p, axis=-1, keepdims=True)
        pv = jnp.dot(p.reshape(n_head * ATT_SUB, ATT_KW).astype(BF16), vs, preferred_element_type=F32)
        pv = pv.reshape(n_head, ATT_SUB, width) / l
        out = pv[0]
        for h in range(1, n_head):
            out = jnp.where(head_of_lane == h, pv[h], out)
        for p_ in range(n_slab):
            o_ref[p_, r0:r0 + ATT_SUB, :] = out[:, p_ * LANES:(p_ + 1) * LANES].astype(o_ref.dtype)


def _attn(slabs, rel_rows, bsz, seq):
    n_grp, n_head = rel_rows.shape[:2]
    n_slab = n_head * HEAD_DIM // LANES
    t = slabs.shape[1]
    nq = seq // ATT_QB
    blk = (n_slab, ATT_QB, LANES)

    def cur(section):
        return lambda p, b, i: (section * n_grp + p, b * nq + i, 0)

    def prev(section):
        return lambda p, b, i: (section * n_grp + p, b * nq + jnp.maximum(i - 1, 0), 0)

    return pl.pallas_call(
        _attn_kernel,
        grid=(n_grp, bsz, nq),
        in_specs=[pl.BlockSpec(blk, cur(0)),
                  pl.BlockSpec(blk, prev(1)), pl.BlockSpec(blk, cur(1)),
                  pl.BlockSpec(blk, prev(2)), pl.BlockSpec(blk, cur(2)),
                  pl.BlockSpec((1, n_head, REL_ROW), lambda p, b, i: (p, 0, 0))],
        out_specs=pl.BlockSpec(blk, lambda p, b, i: (p, b * nq + i, 0)),
        out_shape=jax.ShapeDtypeStruct((n_grp * n_slab, t, LANES), BF16),
        scratch_shapes=[pltpu.VMEM((1 + ATT_QB // ATT_SUB, n_head, ATT_SUB, ATT_KW), F32)],
        compiler_params=_params(("arbitrary", "arbitrary", "arbitrary")),
        name="chunk_attn",
    )(slabs, slabs, slabs, slabs, slabs, rel_rows)


def _rel_rows(rel_bias):
    n_head = rel_bias.shape[0]
    rb = rel_bias.astype(F32)
    far = rb[:, 2 * MAX_REL:]
    n_far = ATT_KW - 2 * MAX_REL
    row = jnp.concatenate([jnp.broadcast_to(far, (n_head, n_far)), rb[:, :0:-1],
                           jnp.broadcast_to(far, (n_head, REL_ROW - ATT_KW))], axis=1)
    return row.reshape(n_head // ATT_HEADS, ATT_HEADS, REL_ROW)


def _gate_kernel(u_ref, v_ref, w_ref, b_ref, o_ref):
    n_grp = u_ref.shape[0]
    tm = u_ref.shape[1]
    ti = lax.broadcasted_iota(jnp.int32, (GATE_BLOCK, GATE_BLOCK), 0)
    si = lax.broadcasted_iota(jnp.int32, (GATE_BLOCK, GATE_BLOCK), 1)
    causal = (ti // CHUNK) >= (si // CHUNK)
    for g in range(n_grp):
        w = jnp.where(causal, w_ref[g], 0.0).astype(BF16)
        for n in range(tm // GATE_BLOCK):
            rows = slice(n * GATE_BLOCK, (n + 1) * GATE_BLOCK)
            z = jnp.dot(w, v_ref[g, rows, :], preferred_element_type=F32) + b_ref[g]
            o_ref[g, rows, :] = (u_ref[g, rows, :].astype(F32) * z).astype(o_ref.dtype)


def _gate(slabs, w_s, b_s):
    n_grp = w_s.shape[0]
    t = slabs.shape[1]
    tm = GATE_TM
    b_bc = jnp.broadcast_to(b_s.astype(F32)[:, :, None], (n_grp, GATE_BLOCK, LANES))
    return pl.pallas_call(
        _gate_kernel,
        grid=(t // tm,),
        in_specs=[pl.BlockSpec((n_grp, tm, LANES), lambda i: (3, i, 0)),
                  pl.BlockSpec((n_grp, tm, LANES), lambda i: (4, i, 0)),
                  pl.BlockSpec((n_grp, GATE_BLOCK, GATE_BLOCK), lambda i: (0, 0, 0)),
                  pl.BlockSpec((n_grp, GATE_BLOCK, LANES), lambda i: (0, 0, 0))],
        out_specs=pl.BlockSpec((n_grp, tm, LANES), lambda i: (0, i, 0)),
        out_shape=jax.ShapeDtypeStruct((n_grp, t, LANES), BF16),
        compiler_params=_params(("parallel",)),
        name="spatial_gate",
    )(slabs, slabs, w_s, b_bc)


def _out_kernel(x_ref, a_ref, b_ref, ga_ref, gb_ref, gt_ref, post_ref, w_ref, o_ref):
    n_slab = a_ref.shape[0]
    width = n_slab * LANES
    tm = x_ref.shape[0]

    def normed_slabs(src_ref, g_ref, r):
        ss = None
        for p in range(n_slab):
            v = src_ref[p, r, :].astype(F32)
            part = jnp.sum(v * v, axis=-1, keepdims=True)
            ss = part if ss is None else ss + part
        inv = lax.rsqrt(ss * (1.0 / width) + EPS)
        return [(src_ref[p, r, :].astype(F32) * inv * g_ref[:, p * LANES:(p + 1) * LANES]).astype(BF16)
                for p in range(n_slab)]

    coef = gt_ref[0] * post_ref[...]
    for r in (slice(0, tm // 2), slice(tm // 2, tm)):
        merged = jnp.concatenate(normed_slabs(a_ref, ga_ref, r) + normed_slabs(b_ref, gb_ref, r), axis=1)
        y = jnp.dot(merged, w_ref[...], preferred_element_type=F32)
        o_ref[r, :] = x_ref[r, :] + coef * _rms(y)


def _out_proj(x2, seq, a_slabs, b_slabs, g_a, g_b, gt, post_g, w_out):
    t, d = x2.shape
    n_slab = a_slabs.shape[0]
    tm = min(OUT_TM, seq)
    per_b = seq // tm
    slab_spec = pl.BlockSpec((n_slab, tm, LANES), lambda i: (0, i, 0))
    half_spec = pl.BlockSpec((1, n_slab * LANES), lambda i: (0, 0))
    return pl.pallas_call(
        _out_kernel,
        grid=(t // tm,),
        in_specs=[pl.BlockSpec((tm, d), lambda i: (i, 0)),
                  slab_spec, slab_spec, half_spec, half_spec,
                  pl.BlockSpec((1, 1, d), lambda i: (i // per_b, 0, 0)),
                  pl.BlockSpec((1, d), lambda i: (0, 0)),
                  pl.BlockSpec((d, d), lambda i: (0, 0), pipeline_mode=pl.Buffered(1))],
        out_specs=pl.BlockSpec((tm, d), lambda i: (i, 0)),
        out_shape=jax.ShapeDtypeStruct((t, d), F32),
        compiler_params=_params(("parallel",)),
        name="out_proj",
    )(x2, a_slabs, b_slabs, g_a, g_b, gt, post_g, w_out)


def kernel(x, c, w_ada, b_ada, ffn1_pre_g, ffn1_post_g, ffn1_w_gu, ffn1_w_down, mix_pre_g, mix_post_g, w_in, rel_bias, ln_v_g, ln_v_b, w_s, b_s, g_out_a, g_out_b, w_out, ffn2_pre_g, ffn2_post_g, ffn2_w_gu, ffn2_w_down):
    bsz, seq, d = x.shape
    depth = w_ada.shape[0]
    assert seq % ATT_QB == 0 and seq % min(PROJ_TM, seq) == 0
    x2 = x.reshape(bsz * seq, d)
    row = lambda v: v.reshape(1, -1)
    for l in range(depth):
        mod = _ada(c, w_ada[l], b_ada[l]).reshape(bsz, N_MOD, 1, d)
        sh1, sc1, gt1, sh2, sc2, gt2, sh3, sc3, gt3 = [mod[:, i] for i in range(N_MOD)]

        jobs = (("gu", ffn2_w_gu[l]), ("cast", ffn2_w_down[l]), ("cast", w_in[l]), ("cast", w_out[l]))
        x2, (wgu2, wd2, w_in_bf, w_out_bf) = _ffn(
            x2, seq, sh1, sc1, gt1, row(ffn1_pre_g[l]), row(ffn1_post_g[l]),
            _pack_gu(ffn1_w_gu[l], FFN_TF), ffn1_w_down[l].astype(BF16), jobs)

        slabs = _proj(x2, seq, sh2, sc2, row(mix_pre_g[l]), row(ln_v_g[l]), row(ln_v_b[l]), w_in_bf)
        out_a = _attn(slabs, _rel_rows(rel_bias[l]), bsz, seq)
        out_b = _gate(slabs, w_s[l], b_s[l])
        x2 = _out_proj(x2, seq, out_a, out_b, row(g_out_a[l]), row(g_out_b[l]), gt2,
                       row(mix_post_g[l]), w_out_bf)

        x2, _ = _ffn(x2, seq, sh3, sc3, gt3, row(ffn2_pre_g[l]), row(ffn2_post_g[l]), wgu2, wd2)
    return x2.reshape(bsz, seq, d)
```

```python
import functools
import math

import jax
import jax.numpy as jnp
from jax import lax
from jax.experimental import pallas as pl
from jax.experimental.pallas import tpu as pltpu

F32 = jnp.float32
BF16 = jnp.bfloat16

EPS = 1e-6
CHUNK = 64
LEFT_CHUNKS = 8
HEAD_DIM = 64
MAX_REL = 128
GATE_BLOCK = 128
N_MOD = 9

LANES = 128
BF16_SUBLANES = 16
F32_SUBLANES = 8
PACK_ROWS = 128
ADA_TN = 2048
VMEM_LIMIT = 58 * 1024 * 1024

FFN_TM = 1024
FFN_TF = 512
PROJ_TM = 1024
ATT_QB = 512
ATT_SUB = 128
ATT_HEADS = 2
ATT_KW = ATT_SUB + LEFT_CHUNKS * CHUNK
REL_ROW = ATT_KW + ATT_SUB
GATE_TM = 1024
OUT_TM = 1024
NEG = -1e30
LOG2E = math.log2(math.e)


def _params(sem, vmem=VMEM_LIMIT):
    return pltpu.CompilerParams(dimension_semantics=sem, vmem_limit_bytes=vmem)


def _rms(v):
    return v * lax.rsqrt(jnp.mean(v * v, axis=-1, keepdims=True) + EPS)


def _modulated(x, gain, scale, shift):
    return _rms(x) * (gain * (1.0 + scale)) + shift


def _gelu(v):
    return 0.5 * v * (1.0 + lax.erf(v * (1.0 / math.sqrt(2.0))))


def _ada_kernel(c_ref, w_ref, b_ref, o_ref):
    c = c_ref[...]
    ca = (c * jax.nn.sigmoid(c)).astype(BF16)
    o_ref[...] = jnp.dot(ca, w_ref[...].astype(BF16), preferred_element_type=F32) + b_ref[...]


def _ada(c, w, b):
    bsz, d = c.shape
    n = w.shape[1]
    rows = F32_SUBLANES
    assert bsz <= rows and n % ADA_TN == 0
    cp = jnp.zeros((rows, d), F32).at[:bsz].set(c)
    tn = ADA_TN
    out = pl.pallas_call(
        _ada_kernel,
        grid=(n // tn,),
        in_specs=[pl.BlockSpec((rows, d), lambda j: (0, 0)),
                  pl.BlockSpec((d, tn), lambda j: (0, j)),
                  pl.BlockSpec((1, tn), lambda j: (0, j))],
        out_specs=pl.BlockSpec((rows, tn), lambda j: (0, j)),
        out_shape=jax.ShapeDtypeStruct((rows, n), F32),
        compiler_params=_params(("arbitrary",)),
        name="adaln",
    )(cp, w, b.reshape(1, n))
    return out[:bsz]


def _pack_gu_rows(src_ref, dst_ref):
    n_f, _, two_tf = dst_ref.shape
    tf = two_tf // 2
    d_ff = src_ref.shape[1] // 2
    for f in range(n_f):
        w = min(tf, d_ff - f * tf)
        dst_ref[f, :, :w] = src_ref[:, f * tf:f * tf + w].astype(BF16)
        dst_ref[f, :, w:2 * w] = src_ref[:, d_ff + f * tf:d_ff + f * tf + w].astype(BF16)
        if w < tf:
            dst_ref[f, :, 2 * w:] = jnp.zeros((dst_ref.shape[1], two_tf - 2 * w), BF16)


def _pack_gu_kernel(src_ref, dst_ref):
    _pack_gu_rows(src_ref, dst_ref)


def _pack_gu(w_gu, tf):
    d, two_f = w_gu.shape
    n_f = pl.cdiv(two_f // 2, tf)
    rb = PACK_ROWS
    return pl.pallas_call(
        _pack_gu_kernel,
        grid=(d // rb,),
        in_specs=[pl.BlockSpec((rb, two_f), lambda r: (r, 0))],
        out_specs=pl.BlockSpec((n_f, rb, 2 * tf), lambda r: (0, r, 0)),
        out_shape=jax.ShapeDtypeStruct((n_f, d, 2 * tf), BF16),
        compiler_params=_params(("parallel",)),
        name="pack_gu",
    )(w_gu)


def _ffn_kernel(*refs, last_width, job_kinds):
    n_job = len(job_kinds)
    x_ref, sh_ref, sc_ref, gt_ref, pre_ref, post_ref, wgu_ref, wd_ref = refs[:8]
    job_src = refs[8:8 + n_job]
    o_ref = refs[8 + n_job]
    job_dst = refs[9 + n_job:9 + 2 * n_job]
    (h_ref,) = refs[9 + 2 * n_job:]
    acc_ref = o_ref
    step = pl.program_id(1)
    n_f = pl.num_programs(1)
    tf = wd_ref.shape[0]
    is_partial = _ffn_tile(pl.program_id(0), step, n_f) == n_f - 1

    def side_jobs():
        for kind, src, dst in zip(job_kinds, job_src, job_dst):
            if kind == "gu":
                _pack_gu_rows(src, dst)
            else:
                dst[...] = src[...].astype(BF16)

    def down_input(h, width):
        gu = jnp.dot(h, wgu_ref[0, :, :2 * width], preferred_element_type=F32)
        g, u = gu[:, :width], gu[:, width:]
        return (g * jax.nn.sigmoid(g) * u).astype(BF16)

    tm = x_ref.shape[0]
    halves = (slice(0, tm // 2), slice(tm // 2, tm))

    def accumulate(width):
        for r in halves:
            a = down_input(h_ref[r, :], width)
            acc_ref[r, :] += jnp.dot(a, wd_ref[:width, :], preferred_element_type=F32)

    @pl.when(step == 0)
    def _():
        for r in halves:
            h = _modulated(x_ref[r, :], pre_ref[...], sc_ref[0], sh_ref[0]).astype(BF16)
            h_ref[r, :] = h
            acc_ref[r, :] = jnp.dot(down_input(h, tf), wd_ref[...], preferred_element_type=F32)
        side_jobs()

    @pl.when((step > 0) & (step < n_f - 1) & jnp.logical_not(is_partial))
    def _():
        accumulate(tf)
        side_jobs()

    @pl.when(is_partial)
    def _():
        accumulate(last_width)
        side_jobs()

    @pl.when(step == n_f - 1)
    def _():
        for r in halves:
            a = down_input(h_ref[r, :], tf)
            y = acc_ref[r, :] + jnp.dot(a, wd_ref[...], preferred_element_type=F32)
            o_ref[r, :] = x_ref[r, :] + (0.5 * gt_ref[0] * post_ref[...]) * _rms(y)
        side_jobs()


def _ffn_tile(i, step, n_f):
    mid = (n_f - 1) // 2
    pos = jnp.where(i % 2 == 0, step, n_f - 1 - step)
    return jnp.where(pos == mid, n_f - 1, jnp.where(pos < mid, pos, pos - 1))


def _job_rows(n_rows, n_steps):
    rb = BF16_SUBLANES
    while n_rows % rb or n_rows // rb > n_steps:
        rb += BF16_SUBLANES
    return rb


def _ffn(x2, seq, sh, sc, gt, pre_g, post_g, wgu, wd, jobs=()):
    t, d = x2.shape
    d_ff = wd.shape[0]
    tm, tf = min(FFN_TM, seq), wgu.shape[2] // 2
    n_f = wgu.shape[0]
    assert n_f >= 3 and n_f == pl.cdiv(d_ff, tf)
    n_steps = (t // tm) * n_f
    per_b = seq // tm
    mod_spec = pl.BlockSpec((1, 1, d), lambda i, f: (i // per_b, 0, 0))
    vec_spec = pl.BlockSpec((1, d), lambda i, f: (0, 0))

    job_in, job_out, job_shape = [], [], []
    for kind, w in jobs:
        rows, cols = w.shape
        rb = _job_rows(rows, n_steps)
        blk = lambda i, f, n=rows // rb: jnp.minimum(i * n_f + f, n - 1)
        job_in.append(pl.BlockSpec((rb, cols), lambda i, f, blk=blk: (blk(i, f), 0)))
        if kind == "gu":
            job_out.append(pl.BlockSpec((n_f, rb, 2 * tf), lambda i, f, blk=blk: (0, blk(i, f), 0)))
            job_shape.append(jax.ShapeDtypeStruct((n_f, rows, 2 * tf), BF16))
        else:
            job_out.append(pl.BlockSpec((rb, cols), lambda i, f, blk=blk: (blk(i, f), 0)))
            job_shape.append(jax.ShapeDtypeStruct((rows, cols), BF16))

    out = pl.pallas_call(
        functools.partial(_ffn_kernel, last_width=d_ff - (n_f - 1) * tf,
                          job_kinds=tuple(kind for kind, _ in jobs)),
        grid=(t // tm, n_f),
        in_specs=[pl.BlockSpec((tm, d), lambda i, f: (i, 0)),
                  mod_spec, mod_spec, mod_spec, vec_spec, vec_spec,
                  pl.BlockSpec((1, d, 2 * tf), lambda i, f: (_ffn_tile(i, f, n_f), 0, 0)),
                  pl.BlockSpec((tf, d), lambda i, f: (_ffn_tile(i, f, n_f), 0))] + job_in,
        out_specs=[pl.BlockSpec((tm, d), lambda i, f: (i, 0))] + job_out,
        out_shape=[jax.ShapeDtypeStruct((t, d), F32)] + job_shape,
        scratch_shapes=[pltpu.VMEM((tm, d), BF16)],
        compiler_params=_params(("arbitrary", "arbitrary")),
        name="ffn_convert" if jobs else "ffn",
    )(x2, sh, sc, gt, pre_g, post_g, wgu, wd, *[w for _, w in jobs])
    return out[0], out[1:]


def _proj_kernel(x_ref, sh_ref, sc_ref, pre_ref, lng_ref, lnb_ref, w_ref, o_ref, h_ref):
    j = pl.program_id(1)
    n_slab = o_ref.shape[0]
    tm = x_ref.shape[0]

    def normed(r):
        h = _modulated(x_ref[r, :], pre_ref[...], sc_ref[0], sh_ref[0]).astype(BF16)
        h_ref[r, :] = h
        return h

    def emit(epilogue, lhs=lambda r: h_ref[r, :]):
        for r in (slice(0, tm // 2), slice(tm // 2, tm)):
            val = epilogue(jnp.dot(lhs(r), w_ref[...], preferred_element_type=F32)).astype(BF16)
            for p in range(n_slab):
                o_ref[p, r, :] = val[:, p * LANES:(p + 1) * LANES]

    def gelu_layer_norm(v):
        ge = _gelu(v)
        xc = ge - jnp.mean(ge, axis=-1, keepdims=True)
        y = xc * lax.rsqrt(jnp.mean(xc * xc, axis=-1, keepdims=True) + EPS)
        return y * lng_ref[...] + lnb_ref[...]

    @pl.when(j == 0)
    def _():
        emit(lambda v: v * (LOG2E / math.sqrt(HEAD_DIM)), lhs=normed)

    @pl.when((j == 1) | (j == 2))
    def _():
        emit(lambda v: v)

    @pl.when(j == 3)
    def _():
        emit(_gelu)

    @pl.when(j == 4)
    def _():
        emit(gelu_layer_norm)


def _proj(x2, seq, sh, sc, pre_g, ln_g, ln_b, w_in):
    t, d = x2.shape
    n = w_in.shape[1]
    tn = n // 5
    n_slab = tn // LANES
    tm = min(PROJ_TM, seq)
    per_b = seq // tm
    mod_spec = pl.BlockSpec((1, 1, d), lambda i, j: (i // per_b, 0, 0))
    vec_spec = pl.BlockSpec((1, d), lambda i, j: (0, 0))
    ln_spec = pl.BlockSpec((1, tn), lambda i, j: (0, 0))
    return pl.pallas_call(
        _proj_kernel,
        grid=(t // tm, 5),
        in_specs=[pl.BlockSpec((tm, d), lambda i, j: (i, 0)),
                  mod_spec, mod_spec, vec_spec, ln_spec, ln_spec,
                  pl.BlockSpec((d, tn), lambda i, j: (0, j))],
        out_specs=pl.BlockSpec((n_slab, tm, LANES), lambda i, j: (j, i, 0)),
        out_shape=jax.ShapeDtypeStruct((5 * n_slab, t, LANES), BF16),
        scratch_shapes=[pltpu.VMEM((tm, d), BF16)],
        compiler_params=_params(("parallel", "arbitrary")),
        name="in_proj",
    )(x2, sh, sc, pre_g, ln_g, ln_b, w_in)


def _attn_kernel(q_ref, kp_ref, kc_ref, vp_ref, vc_ref, rel_ref, o_ref, tab_ref):
    n_sub = ATT_QB // ATT_SUB
    n_slab = q_ref.shape[0]
    n_head = tab_ref.shape[1]
    width = n_slab * LANES

    @pl.when((pl.program_id(1) == 0) & (pl.program_id(2) == 0))
    def _():
        r = lax.broadcasted_iota(jnp.int32, (ATT_SUB, ATT_KW), 0)
        c = lax.broadcasted_iota(jnp.int32, (ATT_SUB, ATT_KW), 1)
        lo = (r // CHUNK) * CHUNK
        in_band = (c >= lo) & (c < lo + (LEFT_CHUNKS + 1) * CHUNK)
        for h in range(n_head):
            base = jnp.broadcast_to(rel_ref[0, h:h + 1, :], (ATT_SUB, REL_ROW))
            toe = pltpu.roll(base, 0, 1, stride=1, stride_axis=0)[:, :ATT_KW] * LOG2E
            t0 = jnp.where(in_band, toe, NEG)
            tab_ref[0, h] = t0
            for i in range(n_sub):
                tab_ref[1 + i, h] = jnp.where(c + i * ATT_SUB >= ATT_QB, t0, NEG)

    qblk = pl.program_id(2)

    def lanes_of(ref, rows=slice(None)):
        return jnp.concatenate([ref[p, rows, :] for p in range(n_slab)], axis=1)

    q = lanes_of(q_ref)
    k = jnp.concatenate([lanes_of(kp_ref), lanes_of(kc_ref)], axis=0)
    v = jnp.concatenate([lanes_of(vp_ref), lanes_of(vc_ref)], axis=0)
    head_of_lane = lax.broadcasted_iota(jnp.int32, (ATT_SUB, width), 1) // HEAD_DIM

    def scores(i):
        qs = q[i * ATT_SUB:(i + 1) * ATT_SUB].astype(F32)
        qh = jnp.concatenate([jnp.where(head_of_lane == h, qs, 0.0) for h in range(n_head)], axis=0)
        ks = k[i * ATT_SUB:i * ATT_SUB + ATT_KW]
        return lax.dot_general(qh.astype(BF16), ks, (((1,), (1,)), ((), ())), preferred_element_type=F32)

    ahead = 2
    pending = [scores(i) for i in range(ahead)]
    for i in range(n_sub):
        r0 = i * ATT_SUB
        vs = v[r0:r0 + ATT_KW]
        s = pending.pop(0)
        if i + ahead < n_sub:
            pending.append(scores(i + ahead))
        s = s.reshape(n_head, ATT_SUB, ATT_KW) + tab_ref[jnp.where(qblk == 0, 1 + i, 0)]
        m = jnp.max(s, axis=-1, keepdims=True)
        p = jnp.exp2(s - m)
        l = jnp.sum(p, axis=-1, keepdims=True)
        pv = jnp.dot(p.reshape(n_head * ATT_SUB, ATT_KW).astype(BF16), vs, preferred_element_type=F32)
        pv = pv.reshape(n_head, ATT_SUB, width) / l
        out = pv[0]
        for h in range(1, n_head):
            out = jnp.where(head_of_lane == h, pv[h], out)
        for p_ in range(n_slab):
            o_ref[p_, r0:r0 + ATT_SUB, :] = out[:, p_ * LANES:(p_ + 1) * LANES].astype(o_ref.dtype)


def _attn(slabs, rel_rows, bsz, seq):
    n_grp, n_head = rel_rows.shape[:2]
    n_slab = n_head * HEAD_DIM // LANES
    t = slabs.shape[1]
    nq = seq // ATT_QB
    blk = (n_slab, ATT_QB, LANES)

    def cur(section):
        return lambda p, b, i: (section * n_grp + p, b * nq + i, 0)

    def prev(section):
        return lambda p, b, i: (section * n_grp + p, b * nq + jnp.maximum(i - 1, 0), 0)

    return pl.pallas_call(
        _attn_kernel,
        grid=(n_grp, bsz, nq),
        in_specs=[pl.BlockSpec(blk, cur(0)),
                  pl.BlockSpec(blk, prev(1)), pl.BlockSpec(blk, cur(1)),
                  pl.BlockSpec(blk, prev(2)), pl.BlockSpec(blk, cur(2)),
                  pl.BlockSpec((1, n_head, REL_ROW), lambda p, b, i: (p, 0, 0))],
        out_specs=pl.BlockSpec(blk, lambda p, b, i: (p, b * nq + i, 0)),
        out_shape=jax.ShapeDtypeStruct((n_grp * n_slab, t, LANES), BF16),
        scratch_shapes=[pltpu.VMEM((1 + ATT_QB // ATT_SUB, n_head, ATT_SUB, ATT_KW), F32)],
        compiler_params=_params(("arbitrary", "arbitrary", "arbitrary")),
        name="chunk_attn",
    )(slabs, slabs, slabs, slabs, slabs, rel_rows)


def _rel_rows(rel_bias):
    n_head = rel_bias.shape[0]
    rb = rel_bias.astype(F32)
    far = rb[:, 2 * MAX_REL:]
    n_far = ATT_KW - 2 * MAX_REL
    row = jnp.concatenate([jnp.broadcast_to(far, (n_head, n_far)), rb[:, :0:-1],
                           jnp.broadcast_to(far, (n_head, REL_ROW - ATT_KW))], axis=1)
    return row.reshape(n_head // ATT_HEADS, ATT_HEADS, REL_ROW)


def _gate_kernel(u_ref, v_ref, w_ref, b_ref, o_ref):
    n_grp = u_ref.shape[0]
    tm = u_ref.shape[1]
    ti = lax.broadcasted_iota(jnp.int32, (GATE_BLOCK, GATE_BLOCK), 0)
    si = lax.broadcasted_iota(jnp.int32, (GATE_BLOCK, GATE_BLOCK), 1)
    causal = (ti // CHUNK) >= (si // CHUNK)
    blocks = [slice(n * GATE_BLOCK, (n + 1) * GATE_BLOCK) for n in range(tm // GATE_BLOCK)]
    for g in range(n_grp):
        w = jnp.where(causal, w_ref[g], 0.0).astype(BF16)
        v_wide = jnp.concatenate([v_ref[g, rows, :] for rows in blocks], axis=1)
        z = jnp.dot(w, v_wide, preferred_element_type=F32)
        for n, rows in enumerate(blocks):
            zn = z[:, n * LANES:(n + 1) * LANES] + b_ref[g]
            o_ref[g, rows, :] = (u_ref[g, rows, :].astype(F32) * zn).astype(o_ref.dtype)


def _gate(slabs, w_s, b_s):
    n_grp = w_s.shape[0]
    t = slabs.shape[1]
    tm = GATE_TM
    b_bc = jnp.broadcast_to(b_s.astype(F32)[:, :, None], (n_grp, GATE_BLOCK, LANES))
    return pl.pallas_call(
        _gate_kernel,
        grid=(t // tm,),
        in_specs=[pl.BlockSpec((n_grp, tm, LANES), lambda i: (3, i, 0)),
                  pl.BlockSpec((n_grp, tm, LANES), lambda i: (4, i, 0)),
                  pl.BlockSpec((n_grp, GATE_BLOCK, GATE_BLOCK), lambda i: (0, 0, 0)),
                  pl.BlockSpec((n_grp, GATE_BLOCK, LANES), lambda i: (0, 0, 0))],
        out_specs=pl.BlockSpec((n_grp, tm, LANES), lambda i: (0, i, 0)),
        out_shape=jax.ShapeDtypeStruct((n_grp, t, LANES), BF16),
        compiler_params=_params(("parallel",)),
        name="spatial_gate",
    )(slabs, slabs, w_s, b_bc)


def _out_kernel(x_ref, a_ref, b_ref, ga_ref, gb_ref, gt_ref, post_ref, w_ref, o_ref):
    n_slab = a_ref.shape[0]
    width = n_slab * LANES
    tm = x_ref.shape[0]

    def normed_slabs(src_ref, g_ref, r):
        ss = None
        for p in range(n_slab):
            v = src_ref[p, r, :].astype(F32)
            part = jnp.sum(v * v, axis=-1, keepdims=True)
            ss = part if ss is None else ss + part
        inv = lax.rsqrt(ss * (1.0 / width) + EPS)
        return [(src_ref[p, r, :].astype(F32) * inv * g_ref[:, p * LANES:(p + 1) * LANES]).astype(BF16)
                for p in range(n_slab)]

    coef = gt_ref[0] * post_ref[...]
    for r in (slice(0, tm // 2), slice(tm // 2, tm)):
        merged = jnp.concatenate(normed_slabs(a_ref, ga_ref, r) + normed_slabs(b_ref, gb_ref, r), axis=1)
        y = jnp.dot(merged, w_ref[...], preferred_element_type=F32)
        o_ref[r, :] = x_ref[r, :] + coef * _rms(y)


def _out_proj(x2, seq, a_slabs, b_slabs, g_a, g_b, gt, post_g, w_out):
    t, d = x2.shape
    n_slab = a_slabs.shape[0]
    tm = min(OUT_TM, seq)
    per_b = seq // tm
    slab_spec = pl.BlockSpec((n_slab, tm, LANES), lambda i: (0, i, 0))
    half_spec = pl.BlockSpec((1, n_slab * LANES), lambda i: (0, 0))
    return pl.pallas_call(
        _out_kernel,
        grid=(t // tm,),
        in_specs=[pl.BlockSpec((tm, d), lambda i: (i, 0)),
                  slab_spec, slab_spec, half_spec, half_spec,
                  pl.BlockSpec((1, 1, d), lambda i: (i // per_b, 0, 0)),
                  pl.BlockSpec((1, d), lambda i: (0, 0)),
                  pl.BlockSpec((d, d), lambda i: (0, 0), pipeline_mode=pl.Buffered(1))],
        out_specs=pl.BlockSpec((tm, d), lambda i: (i, 0)),
        out_shape=jax.ShapeDtypeStruct((t, d), F32),
        compiler_params=_params(("parallel",)),
        name="out_proj",
    )(x2, a_slabs, b_slabs, g_a, g_b, gt, post_g, w_out)


def kernel(x, c, w_ada, b_ada, ffn1_pre_g, ffn1_post_g, ffn1_w_gu, ffn1_w_down, mix_pre_g, mix_post_g, w_in, rel_bias, ln_v_g, ln_v_b, w_s, b_s, g_out_a, g_out_b, w_out, ffn2_pre_g, ffn2_post_g, ffn2_w_gu, ffn2_w_down):
    bsz, seq, d = x.shape
    depth = w_ada.shape[0]
    assert seq % ATT_QB == 0 and seq % min(PROJ_TM, seq) == 0
    x2 = x.reshape(bsz * seq, d)
    row = lambda v: v.reshape(1, -1)
    for l in range(depth):
        mod = _ada(c, w_ada[l], b_ada[l]).reshape(bsz, N_MOD, 1, d)
        sh1, sc1, gt1, sh2, sc2, gt2, sh3, sc3, gt3 = [mod[:, i] for i in range(N_MOD)]

        jobs = (("gu", ffn2_w_gu[l]), ("cast", ffn2_w_down[l]), ("cast", w_in[l]), ("cast", w_out[l]))
        x2, (wgu2, wd2, w_in_bf, w_out_bf) = _ffn(
            x2, seq, sh1, sc1, gt1, row(ffn1_pre_g[l]), row(ffn1_post_g[l]),
            _pack_gu(ffn1_w_gu[l], FFN_TF), ffn1_w_down[l].astype(BF16), jobs)

        slabs = _proj(x2, seq, sh2, sc2, row(mix_pre_g[l]), row(ln_v_g[l]), row(ln_v_b[l]), w_in_bf)
        out_a = _attn(slabs, _rel_rows(rel_bias[l]), bsz, seq)
        out_b = _gate(slabs, w_s[l], b_s[l])
        x2 = _out_proj(x2, seq, out_a, out_b, row(g_out_a[l]), row(g_out_b[l]), gt2,
                       row(mix_post_g[l]), w_out_bf)

        x2, _ = _ffn(x2, seq, sh3, sc3, gt3, row(ffn2_pre_g[l]), row(ffn2_post_g[l]), wgu2, wd2)
    return x2.reshape(bsz, seq, d)
```

```python
import functools
import math

import jax
import jax.numpy as jnp
from jax import lax
from jax.experimental import pallas as pl
from jax.experimental.pallas import tpu as pltpu

F32 = jnp.float32
BF16 = jnp.bfloat16

EPS = 1e-6
CHUNK = 64
LEFT_CHUNKS = 8
HEAD_DIM = 64
MAX_REL = 128
GATE_BLOCK = 128
N_MOD = 9

LANES = 128
BF16_SUBLANES = 16
F32_SUBLANES = 8
PACK_ROWS = 128
ADA_TN = 1024
VMEM_LIMIT = 58 * 1024 * 1024

FFN_TM = 1024
FFN_TF = 512
PROJ_TM = 512
PROJ_SECTIONS = 5
ATT_QB = 512
ATT_SUB = 128
ATT_HEADS = 2
ATT_KW = ATT_SUB + LEFT_CHUNKS * CHUNK
REL_ROW = ATT_KW + ATT_SUB
GATE_TM = 1024
OUT_TM = 1024
NEG = -1e30
LOG2E = math.log2(math.e)


def _params(sem, vmem=VMEM_LIMIT):
    return pltpu.CompilerParams(dimension_semantics=sem, vmem_limit_bytes=vmem)


def _rms(v):
    return v * lax.rsqrt(jnp.mean(v * v, axis=-1, keepdims=True) + EPS)


def _modulated(x, gain, scale, shift):
    return _rms(x) * (gain * (1.0 + scale)) + shift


def _gelu(v):
    return 0.5 * v * (1.0 + lax.erf(v * (1.0 / math.sqrt(2.0))))


def _ada_kernel(c_ref, w_ref, b_ref, o_ref):
    c = c_ref[...]
    ca = (c * jax.nn.sigmoid(c)).astype(BF16)
    o_ref[...] = jnp.dot(ca, w_ref[...].astype(BF16), preferred_element_type=F32) + b_ref[...]


def _ada(c, w, b):
    bsz, d = c.shape
    n = w.shape[1]
    rows = F32_SUBLANES
    assert bsz <= rows and n % ADA_TN == 0
    cp = jnp.zeros((rows, d), F32).at[:bsz].set(c)
    tn = ADA_TN
    out = pl.pallas_call(
        _ada_kernel,
        grid=(n // tn,),
        in_specs=[pl.BlockSpec((rows, d), lambda j: (0, 0)),
                  pl.BlockSpec((d, tn), lambda j: (0, j)),
                  pl.BlockSpec((1, tn), lambda j: (0, j))],
        out_specs=pl.BlockSpec((rows, tn), lambda j: (0, j)),
        out_shape=jax.ShapeDtypeStruct((rows, n), F32),
        compiler_params=_params(("arbitrary",)),
        name="adaln",
    )(cp, w, b.reshape(1, n))
    return out[:bsz]


def _pack_gu_rows(src_ref, dst_ref):
    n_f, _, two_tf = dst_ref.shape
    tf = two_tf // 2
    d_ff = src_ref.shape[1] // 2
    for f in range(n_f):
        w = min(tf, d_ff - f * tf)
        dst_ref[f, :, :w] = src_ref[:, f * tf:f * tf + w].astype(BF16)
        dst_ref[f, :, w:2 * w] = src_ref[:, d_ff + f * tf:d_ff + f * tf + w].astype(BF16)
        if w < tf:
            dst_ref[f, :, 2 * w:] = jnp.zeros((dst_ref.shape[1], two_tf - 2 * w), BF16)


def _pack_gu_kernel(src_ref, dst_ref):
    _pack_gu_rows(src_ref, dst_ref)


def _pack_gu(w_gu, tf):
    d, two_f = w_gu.shape
    n_f = pl.cdiv(two_f // 2, tf)
    rb = PACK_ROWS
    return pl.pallas_call(
        _pack_gu_kernel,
        grid=(d // rb,),
        in_specs=[pl.BlockSpec((rb, two_f), lambda r: (r, 0))],
        out_specs=pl.BlockSpec((n_f, rb, 2 * tf), lambda r: (0, r, 0)),
        out_shape=jax.ShapeDtypeStruct((n_f, d, 2 * tf), BF16),
        compiler_params=_params(("parallel",)),
        name="pack_gu",
    )(w_gu)


def _ffn_kernel(*refs, last_width, job_kinds):
    n_job = len(job_kinds)
    x_ref, sh_ref, sc_ref, gt_ref, pre_ref, post_ref, wgu_ref, wd_ref = refs[:8]
    job_src = refs[8:8 + n_job]
    o_ref = refs[8 + n_job]
    job_dst = refs[9 + n_job:9 + 2 * n_job]
    (h_ref,) = refs[9 + 2 * n_job:]
    acc_ref = o_ref
    step = pl.program_id(1)
    n_f = pl.num_programs(1)
    tf = wd_ref.shape[0]
    is_partial = _ffn_tile(pl.program_id(0), step, n_f) == n_f - 1

    def side_jobs():
        for kind, src, dst in zip(job_kinds, job_src, job_dst):
            if kind == "gu":
                _pack_gu_rows(src, dst)
            else:
                dst[...] = src[...].astype(BF16)

    def down_input(h, width):
        gu = jnp.dot(h, wgu_ref[0, :, :2 * width], preferred_element_type=F32)
        g, u = gu[:, :width], gu[:, width:]
        return (g * jax.nn.sigmoid(g) * u).astype(BF16)

    tm = x_ref.shape[0]
    halves = (slice(0, tm // 2), slice(tm // 2, tm))

    def accumulate(width):
        for r in halves:
            a = down_input(h_ref[r, :], width)
            acc_ref[r, :] += jnp.dot(a, wd_ref[:width, :], preferred_element_type=F32)

    @pl.when(step == 0)
    def _():
        for r in halves:
            h = _modulated(x_ref[r, :], pre_ref[...], sc_ref[0], sh_ref[0]).astype(BF16)
            h_ref[r, :] = h
            acc_ref[r, :] = jnp.dot(down_input(h, tf), wd_ref[...], preferred_element_type=F32)
        side_jobs()

    @pl.when((step > 0) & (step < n_f - 1) & jnp.logical_not(is_partial))
    def _():
        accumulate(tf)
        side_jobs()

    @pl.when(is_partial)
    def _():
        accumulate(last_width)
        side_jobs()

    @pl.when(step == n_f - 1)
    def _():
        for r in halves:
            a = down_input(h_ref[r, :], tf)
            y = acc_ref[r, :] + jnp.dot(a, wd_ref[...], preferred_element_type=F32)
            o_ref[r, :] = x_ref[r, :] + (0.5 * gt_ref[0] * post_ref[...]) * _rms(y)
        side_jobs()


def _ffn_tile(i, step, n_f):
    mid = (n_f - 1) // 2
    pos = jnp.where(i % 2 == 0, step, n_f - 1 - step)
    return jnp.where(pos == mid, n_f - 1, jnp.where(pos < mid, pos, pos - 1))


def _job_rows(n_rows, n_steps):
    rb = BF16_SUBLANES
    while n_rows % rb or n_rows // rb > n_steps:
        rb += BF16_SUBLANES
    return rb


def _ffn(x2, seq, sh, sc, gt, pre_g, post_g, wgu, wd, jobs=()):
    t, d = x2.shape
    d_ff = wd.shape[0]
    tm, tf = min(FFN_TM, seq), wgu.shape[2] // 2
    n_f = wgu.shape[0]
    assert n_f >= 3 and n_f == pl.cdiv(d_ff, tf)
    n_steps = (t // tm) * n_f
    per_b = seq // tm
    mod_spec = pl.BlockSpec((1, 1, d), lambda i, f: (i // per_b, 0, 0))
    vec_spec = pl.BlockSpec((1, d), lambda i, f: (0, 0))

    job_in, job_out, job_shape = [], [], []
    for kind, w in jobs:
        rows, cols = w.shape
        rb = _job_rows(rows, n_steps)
        blk = lambda i, f, n=rows // rb: jnp.minimum(i * n_f + f, n - 1)
        job_in.append(pl.BlockSpec((rb, cols), lambda i, f, blk=blk: (blk(i, f), 0)))
        if kind == "gu":
            job_out.append(pl.BlockSpec((n_f, rb, 2 * tf), lambda i, f, blk=blk: (0, blk(i, f), 0)))
            job_shape.append(jax.ShapeDtypeStruct((n_f, rows, 2 * tf), BF16))
        else:
            job_out.append(pl.BlockSpec((rb, cols), lambda i, f, blk=blk: (blk(i, f), 0)))
            job_shape.append(jax.ShapeDtypeStruct((rows, cols), BF16))

    out = pl.pallas_call(
        functools.partial(_ffn_kernel, last_width=d_ff - (n_f - 1) * tf,
                          job_kinds=tuple(kind for kind, _ in jobs)),
        grid=(t // tm, n_f),
        in_specs=[pl.BlockSpec((tm, d), lambda i, f: (i, 0)),
                  mod_spec, mod_spec, mod_spec, vec_spec, vec_spec,
                  pl.BlockSpec((1, d, 2 * tf), lambda i, f: (_ffn_tile(i, f, n_f), 0, 0)),
                  pl.BlockSpec((tf, d), lambda i, f: (_ffn_tile(i, f, n_f), 0))] + job_in,
        out_specs=[pl.BlockSpec((tm, d), lambda i, f: (i, 0))] + job_out,
        out_shape=[jax.ShapeDtypeStruct((t, d), F32)] + job_shape,
        scratch_shapes=[pltpu.VMEM((tm, d), BF16)],
        compiler_params=_params(("arbitrary", "arbitrary")),
        name="ffn_convert" if jobs else "ffn",
    )(x2, sh, sc, gt, pre_g, post_g, wgu, wd, *[w for _, w in jobs])
    return out[0], out[1:]


def _proj_kernel(x_ref, sh_ref, sc_ref, pre_ref, lng_ref, lnb_ref, w_ref, o_ref):
    n_slab = o_ref.shape[0] // PROJ_SECTIONS
    tn = n_slab * LANES
    h = _modulated(x_ref[...], pre_ref[...], sc_ref[0], sh_ref[0]).astype(BF16)

    def emit(j, epilogue):
        r = jnp.dot(h, w_ref[:, j * tn:(j + 1) * tn], preferred_element_type=F32)
        val = epilogue(r).astype(BF16)
        for p in range(n_slab):
            o_ref[j * n_slab + p] = val[:, p * LANES:(p + 1) * LANES]

    def gelu_layer_norm(v):
        ge = _gelu(v)
        xc = ge - jnp.mean(ge, axis=-1, keepdims=True)
        y = xc * lax.rsqrt(jnp.mean(xc * xc, axis=-1, keepdims=True) + EPS)
        return y * lng_ref[...] + lnb_ref[...]

    emit(4, gelu_layer_norm)
    emit(3, _gelu)
    emit(0, lambda v: v * (LOG2E / math.sqrt(HEAD_DIM)))
    emit(1, lambda v: v)
    emit(2, lambda v: v)


def _proj(x2, seq, sh, sc, pre_g, ln_g, ln_b, w_in):
    t, d = x2.shape
    n = w_in.shape[1]
    tn = n // PROJ_SECTIONS
    n_slab = tn // LANES
    tm = min(PROJ_TM, seq)
    per_b = seq // tm
    mod_spec = pl.BlockSpec((1, 1, d), lambda i: (i // per_b, 0, 0))
    vec_spec = pl.BlockSpec((1, d), lambda i: (0, 0))
    ln_spec = pl.BlockSpec((1, tn), lambda i: (0, 0))
    return pl.pallas_call(
        _proj_kernel,
        grid=(t // tm,),
        in_specs=[pl.BlockSpec((tm, d), lambda i: (i, 0)),
                  mod_spec, mod_spec, vec_spec, ln_spec, ln_spec,
                  pl.BlockSpec((d, n), lambda i: (0, 0), pipeline_mode=pl.Buffered(1))],
        out_specs=pl.BlockSpec((PROJ_SECTIONS * n_slab, tm, LANES), lambda i: (0, i, 0)),
        out_shape=jax.ShapeDtypeStruct((PROJ_SECTIONS * n_slab, t, LANES), BF16),
        compiler_params=_params(("parallel",)),
        name="in_proj",
    )(x2, sh, sc, pre_g, ln_g, ln_b, w_in)


def _attn_kernel(q_ref, kp_ref, kc_ref, vp_ref, vc_ref, rel_ref, o_ref, tab_ref):
    n_sub = ATT_QB // ATT_SUB
    n_slab = q_ref.shape[0]
    n_head = tab_ref.shape[1]
    width = n_slab * LANES

    @pl.when((pl.program_id(1) == 0) & (pl.program_id(2) == 0))
    def _():
        r = lax.broadcasted_iota(jnp.int32, (ATT_SUB, ATT_KW), 0)
        c = lax.broadcasted_iota(jnp.int32, (ATT_SUB, ATT_KW), 1)
        lo = (r // CHUNK) * CHUNK
        in_band = (c >= lo) & (c < lo + (LEFT_CHUNKS + 1) * CHUNK)
        for h in range(n_head):
            row = (rel_ref[0, h:h + 1, :] - rel_ref[0, h:h + 1, 0:1]) * LOG2E
            base = jnp.broadcast_to(row, (ATT_SUB, REL_ROW))
            toe = pltpu.roll(base, 0, 1, stride=1, stride_axis=0)[:, :ATT_KW]
            t0 = jnp.where(in_band, toe, NEG)
            tab_ref[0, h] = t0
            for i in range(n_sub):
                tab_ref[1 + i, h] = jnp.where(c + i * ATT_SUB >= ATT_QB, t0, NEG)

    qblk = pl.program_id(2)

    def lanes_of(ref, rows=slice(None)):
        return jnp.concatenate([ref[p, rows, :] for p in range(n_slab)], axis=1)

    q = lanes_of(q_ref)
    k = jnp.concatenate([lanes_of(kp_ref), lanes_of(kc_ref)], axis=0)
    v = jnp.concatenate([lanes_of(vp_ref), lanes_of(vc_ref)], axis=0)
    head_of_lane = lax.broadcasted_iota(jnp.int32, (ATT_SUB, width), 1) // HEAD_DIM

    def scores(i):
        qs = q[i * ATT_SUB:(i + 1) * ATT_SUB].astype(F32)
        qh = jnp.concatenate([jnp.where(head_of_lane == h, qs, 0.0) for h in range(n_head)], axis=0)
        ks = k[i * ATT_SUB:i * ATT_SUB + ATT_KW]
        return lax.dot_general(qh.astype(BF16), ks, (((1,), (1,)), ((), ())), preferred_element_type=F32)

    ahead = 2
    near = ATT_KW - 2 * MAX_REL

    def attend(sequence_start):
        pending = [scores(i) for i in range(ahead)]
        for i in range(n_sub):
            r0 = i * ATT_SUB
            vs = v[r0:r0 + ATT_KW]
            s = pending.pop(0).reshape(n_head, ATT_SUB, ATT_KW)
            if i + ahead < n_sub:
                pending.append(scores(i + ahead))
            if sequence_start:
                s = s + tab_ref[1 + i]
            else:
                s = jnp.concatenate([s[..., :LANES] + tab_ref[0, :, :, :LANES], s[..., LANES:near],
                                     s[..., near:] + tab_ref[0, :, :, near:]], axis=-1)
            m = jnp.max(s, axis=-1, keepdims=True)
            p = jnp.exp2(s - m)
            l = jnp.sum(p, axis=-1, keepdims=True)
            pv = jnp.dot(p.reshape(n_head * ATT_SUB, ATT_KW).astype(BF16), vs, preferred_element_type=F32)
            pv = pv.reshape(n_head, ATT_SUB, width) / l
            out = pv[0]
            for h in range(1, n_head):
                out = jnp.where(head_of_lane == h, pv[h], out)
            for p_ in range(n_slab):
                o_ref[p_, r0:r0 + ATT_SUB, :] = out[:, p_ * LANES:(p_ + 1) * LANES].astype(o_ref.dtype)

    @pl.when(qblk == 0)
    def _():
        attend(True)

    @pl.when(qblk > 0)
    def _():
        attend(False)


def _attn(slabs, rel_rows, bsz, seq):
    n_grp, n_head = rel_rows.shape[:2]
    n_slab = n_head * HEAD_DIM // LANES
    t = slabs.shape[1]
    nq = seq // ATT_QB
    blk = (n_slab, ATT_QB, LANES)

    def cur(section):
        return lambda p, b, i: (section * n_grp + p, b * nq + i, 0)

    def prev(section):
        return lambda p, b, i: (section * n_grp + p, b * nq + jnp.maximum(i - 1, 0), 0)

    return pl.pallas_call(
        _attn_kernel,
        grid=(n_grp, bsz, nq),
        in_specs=[pl.BlockSpec(blk, cur(0)),
                  pl.BlockSpec(blk, prev(1)), pl.BlockSpec(blk, cur(1)),
                  pl.BlockSpec(blk, prev(2)), pl.BlockSpec(blk, cur(2)),
                  pl.BlockSpec((1, n_head, REL_ROW), lambda p, b, i: (p, 0, 0))],
        out_specs=pl.BlockSpec(blk, lambda p, b, i: (p, b * nq + i, 0)),
        out_shape=jax.ShapeDtypeStruct((n_grp * n_slab, t, LANES), BF16),
        scratch_shapes=[pltpu.VMEM((1 + ATT_QB // ATT_SUB, n_head, ATT_SUB, ATT_KW), F32)],
        compiler_params=_params(("arbitrary", "arbitrary", "arbitrary")),
        name="chunk_attn",
    )(slabs, slabs, slabs, slabs, slabs, rel_rows)


def _rel_rows(rel_bias):
    n_head = rel_bias.shape[0]
    rb = rel_bias.astype(F32)
    far = rb[:, 2 * MAX_REL:]
    n_far = ATT_KW - 2 * MAX_REL
    row = jnp.concatenate([jnp.broadcast_to(far, (n_head, n_far)), rb[:, :0:-1],
                           jnp.broadcast_to(far, (n_head, REL_ROW - ATT_KW))], axis=1)
    return row.reshape(n_head // ATT_HEADS, ATT_HEADS, REL_ROW)


def _gate_kernel(u_ref, v_ref, w_ref, b_ref, o_ref):
    n_grp = u_ref.shape[0]
    tm = u_ref.shape[1]
    ti = lax.broadcasted_iota(jnp.int32, (GATE_BLOCK, GATE_BLOCK), 0)
    si = lax.broadcasted_iota(jnp.int32, (GATE_BLOCK, GATE_BLOCK), 1)
    causal = (ti // CHUNK) >= (si // CHUNK)
    blocks = [slice(n * GATE_BLOCK, (n + 1) * GATE_BLOCK) for n in range(tm // GATE_BLOCK)]
    for g in range(n_grp):
        w = jnp.where(causal, w_ref[g], 0.0).astype(BF16)
        v_wide = jnp.concatenate([v_ref[g, rows, :] for rows in blocks], axis=1)
        z = jnp.dot(w, v_wide, preferred_element_type=F32)
        for n, rows in enumerate(blocks):
            zn = z[:, n * LANES:(n + 1) * LANES] + b_ref[g]
            o_ref[g, rows, :] = (u_ref[g, rows, :].astype(F32) * zn).astype(o_ref.dtype)


def _gate(slabs, w_s, b_s):
    n_grp = w_s.shape[0]
    t = slabs.shape[1]
    tm = GATE_TM
    b_bc = jnp.broadcast_to(b_s.astype(F32)[:, :, None], (n_grp, GATE_BLOCK, LANES))
    return pl.pallas_call(
        _gate_kernel,
        grid=(t // tm,),
        in_specs=[pl.BlockSpec((n_grp, tm, LANES), lambda i: (3, i, 0)),
                  pl.BlockSpec((n_grp, tm, LANES), lambda i: (4, i, 0)),
                  pl.BlockSpec((n_grp, GATE_BLOCK, GATE_BLOCK), lambda i: (0, 0, 0)),
                  pl.BlockSpec((n_grp, GATE_BLOCK, LANES), lambda i: (0, 0, 0))],
        out_specs=pl.BlockSpec((n_grp, tm, LANES), lambda i: (0, i, 0)),
        out_shape=jax.ShapeDtypeStruct((n_grp, t, LANES), BF16),
        compiler_params=_params(("parallel",)),
        name="spatial_gate",
    )(slabs, slabs, w_s, b_bc)


def _out_kernel(x_ref, a_ref, b_ref, ga_ref, gb_ref, gt_ref, post_ref, w_ref, o_ref):
    n_slab = a_ref.shape[0]
    width = n_slab * LANES
    tm = x_ref.shape[0]

    def normed_slabs(src_ref, g_ref, r):
        ss = None
        for p in range(n_slab):
            v = src_ref[p, r, :].astype(F32)
            part = jnp.sum(v * v, axis=-1, keepdims=True)
            ss = part if ss is None else ss + part
        inv = lax.rsqrt(ss * (1.0 / width) + EPS)
        return [(src_ref[p, r, :].astype(F32) * inv * g_ref[:, p * LANES:(p + 1) * LANES]).astype(BF16)
                for p in range(n_slab)]

    coef = gt_ref[0] * post_ref[...]
    for r in (slice(0, tm // 2), slice(tm // 2, tm)):
        merged = jnp.concatenate(normed_slabs(a_ref, ga_ref, r) + normed_slabs(b_ref, gb_ref, r), axis=1)
        y = jnp.dot(merged, w_ref[...], preferred_element_type=F32)
        o_ref[r, :] = x_ref[r, :] + coef * _rms(y)


def _out_proj(x2, seq, a_slabs, b_slabs, g_a, g_b, gt, post_g, w_out):
    t, d = x2.shape
    n_slab = a_slabs.shape[0]
    tm = min(OUT_TM, seq)
    per_b = seq // tm
    slab_spec = pl.BlockSpec((n_slab, tm, LANES), lambda i: (0, i, 0))
    half_spec = pl.BlockSpec((1, n_slab * LANES), lambda i: (0, 0))
    return pl.pallas_call(
        _out_kernel,
        grid=(t // tm,),
        in_specs=[pl.BlockSpec((tm, d), lambda i: (i, 0)),
                  slab_spec, slab_spec, half_spec, half_spec,
                  pl.BlockSpec((1, 1, d), lambda i: (i // per_b, 0, 0)),
                  pl.BlockSpec((1, d), lambda i: (0, 0)),
                  pl.BlockSpec((d, d), lambda i: (0, 0), pipeline_mode=pl.Buffered(1))],
        out_specs=pl.BlockSpec((tm, d), lambda i: (i, 0)),
        out_shape=jax.ShapeDtypeStruct((t, d), F32),
        compiler_params=_params(("parallel",)),
        name="out_proj",
    )(x2, a_slabs, b_slabs, g_a, g_b, gt, post_g, w_out)


def kernel(x, c, w_ada, b_ada, ffn1_pre_g, ffn1_post_g, ffn1_w_gu, ffn1_w_down, mix_pre_g, mix_post_g, w_in, rel_bias, ln_v_g, ln_v_b, w_s, b_s, g_out_a, g_out_b, w_out, ffn2_pre_g, ffn2_post_g, ffn2_w_gu, ffn2_w_down):
    bsz, seq, d = x.shape
    depth = w_ada.shape[0]
    assert seq % ATT_QB == 0 and seq % min(PROJ_TM, seq) == 0
    x2 = x.reshape(bsz * seq, d)
    row = lambda v: v.reshape(1, -1)
    for l in range(depth):
        mod = _ada(c, w_ada[l], b_ada[l]).reshape(bsz, N_MOD, 1, d)
        sh1, sc1, gt1, sh2, sc2, gt2, sh3, sc3, gt3 = [mod[:, i] for i in range(N_MOD)]

        jobs = (("gu", ffn2_w_gu[l]), ("cast", ffn2_w_down[l]), ("cast", w_in[l]), ("cast", w_out[l]))
        x2, (wgu2, wd2, w_in_bf, w_out_bf) = _ffn(
            x2, seq, sh1, sc1, gt1, row(ffn1_pre_g[l]), row(ffn1_post_g[l]),
            _pack_gu(ffn1_w_gu[l], FFN_TF), ffn1_w_down[l].astype(BF16), jobs)

        slabs = _proj(x2, seq, sh2, sc2, row(mix_pre_g[l]), row(ln_v_g[l]), row(ln_v_b[l]), w_in_bf)
        out_a = _attn(slabs, _rel_rows(rel_bias[l]), bsz, seq)
        out_b = _gate(slabs, w_s[l], b_s[l])
        x2 = _out_proj(x2, seq, out_a, out_b, row(g_out_a[l]), row(g_out_b[l]), gt2,
                       row(mix_post_g[l]), w_out_bf)

        x2, _ = _ffn(x2, seq, sh3, sc3, gt3, row(ffn2_pre_g[l]), row(ffn2_post_g[l]), wgu2, wd2)
    return x2.reshape(bsz, seq, d)
```

```python
import functools
import math

import jax
import jax.numpy as jnp
from jax import lax
from jax.experimental import pallas as pl
from jax.experimental.pallas import tpu as pltpu

F32 = jnp.float32
BF16 = jnp.bfloat16

EPS = 1e-6
CHUNK = 64
LEFT_CHUNKS = 8
HEAD_DIM = 64
MAX_REL = 128
GATE_BLOCK = 128
N_MOD = 9

LANES = 128
BF16_SUBLANES = 16
F32_SUBLANES = 8
PACK_ROWS = 128
ADA_TN = 1024
VMEM_LIMIT = 58 * 1024 * 1024

FFN_TM = 1024
FFN_TF = 512
PROJ_TM = 512
PROJ_SECTIONS = 5
ATT_QB = 512
ATT_SUB = 128
ATT_HEADS = 8
ATT_KW = ATT_SUB + LEFT_CHUNKS * CHUNK
REL_ROW = ATT_KW + ATT_SUB
GATE_TM = 1024
OUT_TM = 1024
NEG = -1e30
LOG2E = math.log2(math.e)


def _params(sem, vmem=VMEM_LIMIT):
    return pltpu.CompilerParams(dimension_semantics=sem, vmem_limit_bytes=vmem)


def _rms(v):
    return v * lax.rsqrt(jnp.mean(v * v, axis=-1, keepdims=True) + EPS)


def _modulated(x, gain, scale, shift):
    return _rms(x) * (gain * (1.0 + scale)) + shift


def _gelu(v):
    return 0.5 * v * (1.0 + lax.erf(v * (1.0 / math.sqrt(2.0))))


def _ada_kernel(c_ref, w_ref, b_ref, o_ref):
    c = c_ref[...]
    ca = (c * jax.nn.sigmoid(c)).astype(BF16)
    o_ref[...] = jnp.dot(ca, w_ref[...].astype(BF16), preferred_element_type=F32) + b_ref[...]


def _ada(c, w, b):
    bsz, d = c.shape
    n = w.shape[1]
    rows = F32_SUBLANES
    assert bsz <= rows and n % ADA_TN == 0
    cp = jnp.zeros((rows, d), F32).at[:bsz].set(c)
    tn = ADA_TN
    out = pl.pallas_call(
        _ada_kernel,
        grid=(n // tn,),
        in_specs=[pl.BlockSpec((rows, d), lambda j: (0, 0)),
                  pl.BlockSpec((d, tn), lambda j: (0, j)),
                  pl.BlockSpec((1, tn), lambda j: (0, j))],
        out_specs=pl.BlockSpec((rows, tn), lambda j: (0, j)),
        out_shape=jax.ShapeDtypeStruct((rows, n), F32),
        compiler_params=_params(("arbitrary",)),
        name="adaln",
    )(cp, w, b.reshape(1, n))
    return out[:bsz]


def _pack_gu_rows(src_ref, dst_ref):
    n_f, _, two_tf = dst_ref.shape
    tf = two_tf // 2
    d_ff = src_ref.shape[1] // 2
    for f in range(n_f):
        w = min(tf, d_ff - f * tf)
        dst_ref[f, :, :w] = src_ref[:, f * tf:f * tf + w].astype(BF16)
        dst_ref[f, :, w:2 * w] = src_ref[:, d_ff + f * tf:d_ff + f * tf + w].astype(BF16)
        if w < tf:
            dst_ref[f, :, 2 * w:] = jnp.zeros((dst_ref.shape[1], two_tf - 2 * w), BF16)


def _pack_gu_kernel(src_ref, dst_ref):
    _pack_gu_rows(src_ref, dst_ref)


def _pack_gu(w_gu, tf):
    d, two_f = w_gu.shape
    n_f = pl.cdiv(two_f // 2, tf)
    rb = PACK_ROWS
    return pl.pallas_call(
        _pack_gu_kernel,
        grid=(d // rb,),
        in_specs=[pl.BlockSpec((rb, two_f), lambda r: (r, 0))],
        out_specs=pl.BlockSpec((n_f, rb, 2 * tf), lambda r: (0, r, 0)),
        out_shape=jax.ShapeDtypeStruct((n_f, d, 2 * tf), BF16),
        compiler_params=_params(("parallel",)),
        name="pack_gu",
    )(w_gu)


def _ffn_kernel(*refs, last_width, job_kinds):
    n_job = len(job_kinds)
    x_ref, sh_ref, sc_ref, gt_ref, pre_ref, post_ref, wgu_ref, wd_ref = refs[:8]
    job_src = refs[8:8 + n_job]
    o_ref = refs[8 + n_job]
    job_dst = refs[9 + n_job:9 + 2 * n_job]
    (h_ref,) = refs[9 + 2 * n_job:]
    acc_ref = o_ref
    step = pl.program_id(1)
    n_f = pl.num_programs(1)
    tf = wd_ref.shape[0]
    is_partial = _ffn_tile(pl.program_id(0), step, n_f) == n_f - 1

    def side_jobs():
        for kind, src, dst in zip(job_kinds, job_src, job_dst):
            if kind == "gu":
                _pack_gu_rows(src, dst)
            else:
                dst[...] = src[...].astype(BF16)

    def down_input(h, width):
        gu = jnp.dot(h, wgu_ref[0, :, :2 * width], preferred_element_type=F32)
        g, u = gu[:, :width], gu[:, width:]
        return (g * jax.nn.sigmoid(g) * u).astype(BF16)

    tm = x_ref.shape[0]
    halves = (slice(0, tm // 2), slice(tm // 2, tm))

    def accumulate(width):
        for r in halves:
            a = down_input(h_ref[r, :], width)
            acc_ref[r, :] += jnp.dot(a, wd_ref[:width, :], preferred_element_type=F32)

    @pl.when(step == 0)
    def _():
        for r in halves:
            h = _modulated(x_ref[r, :], pre_ref[...], sc_ref[0], sh_ref[0]).astype(BF16)
            h_ref[r, :] = h
            acc_ref[r, :] = jnp.dot(down_input(h, tf), wd_ref[...], preferred_element_type=F32)
        side_jobs()

    @pl.when((step > 0) & (step < n_f - 1) & jnp.logical_not(is_partial))
    def _():
        accumulate(tf)
        side_jobs()

    @pl.when(is_partial)
    def _():
        accumulate(last_width)
        side_jobs()

    @pl.when(step == n_f - 1)
    def _():
        for r in halves:
            a = down_input(h_ref[r, :], tf)
            y = acc_ref[r, :] + jnp.dot(a, wd_ref[...], preferred_element_type=F32)
            o_ref[r, :] = x_ref[r, :] + (0.5 * gt_ref[0] * post_ref[...]) * _rms(y)
        side_jobs()


def _ffn_tile(i, step, n_f):
    mid = (n_f - 1) // 2
    pos = jnp.where(i % 2 == 0, step, n_f - 1 - step)
    return jnp.where(pos == mid, n_f - 1, jnp.where(pos < mid, pos, pos - 1))


def _job_rows(n_rows, n_steps):
    rb = BF16_SUBLANES
    while n_rows % rb or n_rows // rb > n_steps:
        rb += BF16_SUBLANES
    return rb


def _ffn(x2, seq, sh, sc, gt, pre_g, post_g, wgu, wd, jobs=()):
    t, d = x2.shape
    d_ff = wd.shape[0]
    tm, tf = min(FFN_TM, seq), wgu.shape[2] // 2
    n_f = wgu.shape[0]
    assert n_f >= 3 and n_f == pl.cdiv(d_ff, tf)
    n_steps = (t // tm) * n_f
    per_b = seq // tm
    mod_spec = pl.BlockSpec((1, 1, d), lambda i, f: (i // per_b, 0, 0))
    vec_spec = pl.BlockSpec((1, d), lambda i, f: (0, 0))

    job_in, job_out, job_shape = [], [], []
    for kind, w in jobs:
        rows, cols = w.shape
        rb = _job_rows(rows, n_steps)
        blk = lambda i, f, n=rows // rb: jnp.minimum(i * n_f + f, n - 1)
        job_in.append(pl.BlockSpec((rb, cols), lambda i, f, blk=blk: (blk(i, f), 0)))
        if kind == "gu":
            job_out.append(pl.BlockSpec((n_f, rb, 2 * tf), lambda i, f, blk=blk: (0, blk(i, f), 0)))
            job_shape.append(jax.ShapeDtypeStruct((n_f, rows, 2 * tf), BF16))
        else:
            job_out.append(pl.BlockSpec((rb, cols), lambda i, f, blk=blk: (blk(i, f), 0)))
            job_shape.append(jax.ShapeDtypeStruct((rows, cols), BF16))

    out = pl.pallas_call(
        functools.partial(_ffn_kernel, last_width=d_ff - (n_f - 1) * tf,
                          job_kinds=tuple(kind for kind, _ in jobs)),
        grid=(t // tm, n_f),
        in_specs=[pl.BlockSpec((tm, d), lambda i, f: (i, 0)),
                  mod_spec, mod_spec, mod_spec, vec_spec, vec_spec,
                  pl.BlockSpec((1, d, 2 * tf), lambda i, f: (_ffn_tile(i, f, n_f), 0, 0)),
                  pl.BlockSpec((tf, d), lambda i, f: (_ffn_tile(i, f, n_f), 0))] + job_in,
        out_specs=[pl.BlockSpec((tm, d), lambda i, f: (i, 0))] + job_out,
        out_shape=[jax.ShapeDtypeStruct((t, d), F32)] + job_shape,
        scratch_shapes=[pltpu.VMEM((tm, d), BF16)],
        compiler_params=_params(("arbitrary", "arbitrary")),
        name="ffn_convert" if jobs else "ffn",
    )(x2, sh, sc, gt, pre_g, post_g, wgu, wd, *[w for _, w in jobs])
    return out[0], out[1:]


def _proj_kernel(x_ref, sh_ref, sc_ref, pre_ref, lng_ref, lnb_ref, w_ref, o_ref):
    n_slab = o_ref.shape[0] // PROJ_SECTIONS
    tn = n_slab * LANES
    h = _modulated(x_ref[...], pre_ref[...], sc_ref[0], sh_ref[0]).astype(BF16)

    def emit(j, epilogue):
        r = jnp.dot(h, w_ref[:, j * tn:(j + 1) * tn], preferred_element_type=F32)
        val = epilogue(r).astype(BF16)
        for p in range(n_slab):
            o_ref[j * n_slab + p] = val[:, p * LANES:(p + 1) * LANES]

    def gelu_layer_norm(v):
        ge = _gelu(v)
        xc = ge - jnp.mean(ge, axis=-1, keepdims=True)
        y = xc * lax.rsqrt(jnp.mean(xc * xc, axis=-1, keepdims=True) + EPS)
        return y * lng_ref[...] + lnb_ref[...]

    emit(4, gelu_layer_norm)
    emit(3, _gelu)
    emit(0, lambda v: v * (LOG2E / math.sqrt(HEAD_DIM)))
    emit(1, lambda v: v)
    emit(2, lambda v: v)


def _proj(x2, seq, sh, sc, pre_g, ln_g, ln_b, w_in):
    t, d = x2.shape
    n = w_in.shape[1]
    tn = n // PROJ_SECTIONS
    n_slab = tn // LANES
    tm = min(PROJ_TM, seq)
    per_b = seq // tm
    mod_spec = pl.BlockSpec((1, 1, d), lambda i: (i // per_b, 0, 0))
    vec_spec = pl.BlockSpec((1, d), lambda i: (0, 0))
    ln_spec = pl.BlockSpec((1, tn), lambda i: (0, 0))
    return pl.pallas_call(
        _proj_kernel,
        grid=(t // tm,),
        in_specs=[pl.BlockSpec((tm, d), lambda i: (i, 0)),
                  mod_spec, mod_spec, vec_spec, ln_spec, ln_spec,
                  pl.BlockSpec((d, n), lambda i: (0, 0), pipeline_mode=pl.Buffered(1))],
        out_specs=pl.BlockSpec((PROJ_SECTIONS * n_slab, tm, LANES), lambda i: (0, i, 0)),
        out_shape=jax.ShapeDtypeStruct((PROJ_SECTIONS * n_slab, t, LANES), BF16),
        compiler_params=_params(("parallel",)),
        name="in_proj",
    )(x2, sh, sc, pre_g, ln_g, ln_b, w_in)


def _attn_kernel(q_ref, kp_ref, kc_ref, vp_ref, vc_ref, rel_ref, o_ref, tab_ref):
    n_sub = ATT_QB // ATT_SUB
    n_slab = q_ref.shape[0]
    n_head = tab_ref.shape[1]

    @pl.when((pl.program_id(1) == 0) & (pl.program_id(2) == 0))
    def _():
        r = lax.broadcasted_iota(jnp.int32, (ATT_SUB, ATT_KW), 0)
        c = lax.broadcasted_iota(jnp.int32, (ATT_SUB, ATT_KW), 1)
        lo = (r // CHUNK) * CHUNK
        in_band = (c >= lo) & (c < lo + (LEFT_CHUNKS + 1) * CHUNK)
        for h in range(n_head):
            row = (rel_ref[0, h:h + 1, :] - rel_ref[0, h:h + 1, 0:1]) * LOG2E
            base = jnp.broadcast_to(row, (ATT_SUB, REL_ROW))
            toe = pltpu.roll(base, 0, 1, stride=1, stride_axis=0)[:, :ATT_KW]
            t0 = jnp.where(in_band, toe, NEG)
            tab_ref[0, h] = t0
            for i in range(n_sub):
                tab_ref[1 + i, h] = jnp.where(c + i * ATT_SUB >= ATT_QB, t0, NEG)

    qblk = pl.program_id(2)

    pair = LANES // HEAD_DIM
    upper = lax.broadcasted_iota(jnp.int32, (ATT_SUB, LANES), 1) >= HEAD_DIM
    ahead = 2
    near = ATT_KW - 2 * MAX_REL

    def attend_slab(slab, sequence_start):
        q = q_ref[slab]
        k = jnp.concatenate([kp_ref[slab], kc_ref[slab]], axis=0)
        v = jnp.concatenate([vp_ref[slab], vc_ref[slab]], axis=0)
        heads = slice(slab * pair, (slab + 1) * pair)

        def scores(i):
            qs = q[i * ATT_SUB:(i + 1) * ATT_SUB].astype(F32)
            qh = jnp.concatenate([jnp.where(upper, 0.0, qs), jnp.where(upper, qs, 0.0)], axis=0)
            ks = k[i * ATT_SUB:i * ATT_SUB + ATT_KW]
            return lax.dot_general(qh.astype(BF16), ks, (((1,), (1,)), ((), ())), preferred_element_type=F32)

        pending = [scores(i) for i in range(ahead)]
        for i in range(n_sub):
            r0 = i * ATT_SUB
            vs = v[r0:r0 + ATT_KW]
            s = pending.pop(0).reshape(pair, ATT_SUB, ATT_KW)
            if i + ahead < n_sub:
                pending.append(scores(i + ahead))
            if sequence_start:
                s = s + tab_ref[1 + i, heads]
            else:
                s = jnp.concatenate([s[..., :LANES] + tab_ref[0, heads, :, :LANES], s[..., LANES:near],
                                     s[..., near:] + tab_ref[0, heads, :, near:]], axis=-1)
            m = jnp.max(s, axis=-1, keepdims=True)
            p = jnp.exp2(s - m)
            l = jnp.sum(p, axis=-1, keepdims=True)
            pv = jnp.dot(p.reshape(pair * ATT_SUB, ATT_KW).astype(BF16), vs, preferred_element_type=F32)
            pv = pv.reshape(pair, ATT_SUB, LANES) / l
            o_ref[slab, r0:r0 + ATT_SUB, :] = jnp.where(upper, pv[1], pv[0]).astype(o_ref.dtype)

    def attend(sequence_start):
        for slab in range(n_slab):
            attend_slab(slab, sequence_start)

    @pl.when(qblk == 0)
    def _():
        attend(True)

    @pl.when(qblk > 0)
    def _():
        attend(False)


def _attn(slabs, rel_rows, bsz, seq):
    n_grp, n_head = rel_rows.shape[:2]
    n_slab = n_head * HEAD_DIM // LANES
    t = slabs.shape[1]
    nq = seq // ATT_QB
    blk = (n_slab, ATT_QB, LANES)

    def cur(section):
        return lambda p, b, i: (section * n_grp + p, b * nq + i, 0)

    def prev(section):
        return lambda p, b, i: (section * n_grp + p, b * nq + jnp.maximum(i - 1, 0), 0)

    return pl.pallas_call(
        _attn_kernel,
        grid=(n_grp, bsz, nq),
        in_specs=[pl.BlockSpec(blk, cur(0)),
                  pl.BlockSpec(blk, prev(1)), pl.BlockSpec(blk, cur(1)),
                  pl.BlockSpec(blk, prev(2)), pl.BlockSpec(blk, cur(2)),
                  pl.BlockSpec((1, n_head, REL_ROW), lambda p, b, i: (p, 0, 0))],
        out_specs=pl.BlockSpec(blk, lambda p, b, i: (p, b * nq + i, 0)),
        out_shape=jax.ShapeDtypeStruct((n_grp * n_slab, t, LANES), BF16),
        scratch_shapes=[pltpu.VMEM((1 + ATT_QB // ATT_SUB, n_head, ATT_SUB, ATT_KW), F32)],
        compiler_params=_params(("arbitrary", "arbitrary", "arbitrary")),
        name="chunk_attn",
    )(slabs, slabs, slabs, slabs, slabs, rel_rows)


def _rel_rows(rel_bias):
    n_head = rel_bias.shape[0]
    rb = rel_bias.astype(F32)
    far = rb[:, 2 * MAX_REL:]
    n_far = ATT_KW - 2 * MAX_REL
    row = jnp.concatenate([jnp.broadcast_to(far, (n_head, n_far)), rb[:, :0:-1],
                           jnp.broadcast_to(far, (n_head, REL_ROW - ATT_KW))], axis=1)
    return row.reshape(n_head // ATT_HEADS, ATT_HEADS, REL_ROW)


def _gate_kernel(u_ref, v_ref, w_ref, b_ref, o_ref):
    n_grp = u_ref.shape[0]
    tm = u_ref.shape[1]
    ti = lax.broadcasted_iota(jnp.int32, (GATE_BLOCK, GATE_BLOCK), 0)
    si = lax.broadcasted_iota(jnp.int32, (GATE_BLOCK, GATE_BLOCK), 1)
    causal = (ti // CHUNK) >= (si // CHUNK)
    blocks = [slice(n * GATE_BLOCK, (n + 1) * GATE_BLOCK) for n in range(tm // GATE_BLOCK)]
    for g in range(n_grp):
        w = jnp.where(causal, w_ref[g], 0.0).astype(BF16)
        v_wide = jnp.concatenate([v_ref[g, rows, :] for rows in blocks], axis=1)
        z = jnp.dot(w, v_wide, preferred_element_type=F32)
        for n, rows in enumerate(blocks):
            zn = z[:, n * LANES:(n + 1) * LANES] + b_ref[g]
            o_ref[g, rows, :] = (u_ref[g, rows, :].astype(F32) * zn).astype(o_ref.dtype)


def _gate(slabs, w_s, b_s):
    n_grp = w_s.shape[0]
    t = slabs.shape[1]
    tm = GATE_TM
    b_bc = jnp.broadcast_to(b_s.astype(F32)[:, :, None], (n_grp, GATE_BLOCK, LANES))
    return pl.pallas_call(
        _gate_kernel,
        grid=(t // tm,),
        in_specs=[pl.BlockSpec((n_grp, tm, LANES), lambda i: (3, i, 0)),
                  pl.BlockSpec((n_grp, tm, LANES), lambda i: (4, i, 0)),
                  pl.BlockSpec((n_grp, GATE_BLOCK, GATE_BLOCK), lambda i: (0, 0, 0)),
                  pl.BlockSpec((n_grp, GATE_BLOCK, LANES), lambda i: (0, 0, 0))],
        out_specs=pl.BlockSpec((n_grp, tm, LANES), lambda i: (0, i, 0)),
        out_shape=jax.ShapeDtypeStruct((n_grp, t, LANES), BF16),
        compiler_params=_params(("parallel",)),
        name="spatial_gate",
    )(slabs, slabs, w_s, b_bc)


def _out_kernel(x_ref, a_ref, b_ref, ga_ref, gb_ref, gt_ref, post_ref, w_ref, o_ref):
    n_slab = a_ref.shape[0]
    width = n_slab * LANES
    tm = x_ref.shape[0]

    def normed_slabs(src_ref, g_ref, r):
        ss = None
        for p in range(n_slab):
            v = src_ref[p, r, :].astype(F32)
            part = jnp.sum(v * v, axis=-1, keepdims=True)
            ss = part if ss is None else ss + part
        inv = lax.rsqrt(ss * (1.0 / width) + EPS)
        return [(src_ref[p, r, :].astype(F32) * inv * g_ref[:, p * LANES:(p + 1) * LANES]).astype(BF16)
                for p in range(n_slab)]

    coef = gt_ref[0] * post_ref[...]
    for r in (slice(0, tm // 2), slice(tm // 2, tm)):
        merged = jnp.concatenate(normed_slabs(a_ref, ga_ref, r) + normed_slabs(b_ref, gb_ref, r), axis=1)
        y = jnp.dot(merged, w_ref[...], preferred_element_type=F32)
        o_ref[r, :] = x_ref[r, :] + coef * _rms(y)


def _out_proj(x2, seq, a_slabs, b_slabs, g_a, g_b, gt, post_g, w_out):
    t, d = x2.shape
    n_slab = a_slabs.shape[0]
    tm = min(OUT_TM, seq)
    per_b = seq // tm
    slab_spec = pl.BlockSpec((n_slab, tm, LANES), lambda i: (0, i, 0))
    half_spec = pl.BlockSpec((1, n_slab * LANES), lambda i: (0, 0))
    return pl.pallas_call(
        _out_kernel,
        grid=(t // tm,),
        in_specs=[pl.BlockSpec((tm, d), lambda i: (i, 0)),
                  slab_spec, slab_spec, half_spec, half_spec,
                  pl.BlockSpec((1, 1, d), lambda i: (i // per_b, 0, 0)),
                  pl.BlockSpec((1, d), lambda i: (0, 0)),
                  pl.BlockSpec((d, d), lambda i: (0, 0), pipeline_mode=pl.Buffered(1))],
        out_specs=pl.BlockSpec((tm, d), lambda i: (i, 0)),
        out_shape=jax.ShapeDtypeStruct((t, d), F32),
        compiler_params=_params(("parallel",)),
        name="out_proj",
    )(x2, a_slabs, b_slabs, g_a, g_b, gt, post_g, w_out)


def kernel(x, c, w_ada, b_ada, ffn1_pre_g, ffn1_post_g, ffn1_w_gu, ffn1_w_down, mix_pre_g, mix_post_g, w_in, rel_bias, ln_v_g, ln_v_b, w_s, b_s, g_out_a, g_out_b, w_out, ffn2_pre_g, ffn2_post_g, ffn2_w_gu, ffn2_w_down):
    bsz, seq, d = x.shape
    depth = w_ada.shape[0]
    assert seq % ATT_QB == 0 and seq % min(PROJ_TM, seq) == 0
    x2 = x.reshape(bsz * seq, d)
    row = lambda v: v.reshape(1, -1)
    for l in range(depth):
        mod = _ada(c, w_ada[l], b_ada[l]).reshape(bsz, N_MOD, 1, d)
        sh1, sc1, gt1, sh2, sc2, gt2, sh3, sc3, gt3 = [mod[:, i] for i in range(N_MOD)]

        jobs = (("gu", ffn2_w_gu[l]), ("cast", ffn2_w_down[l]), ("cast", w_in[l]), ("cast", w_out[l]))
        x2, (wgu2, wd2, w_in_bf, w_out_bf) = _ffn(
            x2, seq, sh1, sc1, gt1, row(ffn1_pre_g[l]), row(ffn1_post_g[l]),
            _pack_gu(ffn1_w_gu[l], FFN_TF), ffn1_w_down[l].astype(BF16), jobs)

        slabs = _proj(x2, seq, sh2, sc2, row(mix_pre_g[l]), row(ln_v_g[l]), row(ln_v_b[l]), w_in_bf)
        out_a = _attn(slabs, _rel_rows(rel_bias[l]), bsz, seq)
        out_b = _gate(slabs, w_s[l], b_s[l])
        x2 = _out_proj(x2, seq, out_a, out_b, row(g_out_a[l]), row(g_out_b[l]), gt2,
                       row(mix_post_g[l]), w_out_bf)

        x2, _ = _ffn(x2, seq, sh3, sc3, gt3, row(ffn2_pre_g[l]), row(ffn2_post_g[l]), wgu2, wd2)
    return x2.reshape(bsz, seq, d)
```

```python
import functools
import math

import jax
import jax.numpy as jnp
from jax import lax
from jax.experimental import pallas as pl
from jax.experimental.pallas import tpu as pltpu

F32 = jnp.float32
BF16 = jnp.bfloat16

EPS = 1e-6
CHUNK = 64
LEFT_CHUNKS = 8
HEAD_DIM = 64
MAX_REL = 128
GATE_BLOCK = 128
N_MOD = 9

LANES = 128
BF16_SUBLANES = 16
F32_SUBLANES = 8
PACK_ROWS = 128
ADA_TN = 1024
VMEM_LIMIT = 58 * 1024 * 1024

FFN_TM = 1024
FFN_TF = 512
PROJ_TM = 512
PROJ_SECTIONS = 5
ATT_QB = 512
ATT_SUB = 128
ATT_HEADS = 16
ATT_KW = ATT_SUB + LEFT_CHUNKS * CHUNK
REL_ROW = ATT_KW + ATT_SUB
GATE_TM = 1024
OUT_TM = 1024
NEG = -1e30
LOG2E = math.log2(math.e)


def _params(sem, vmem=VMEM_LIMIT):
    return pltpu.CompilerParams(dimension_semantics=sem, vmem_limit_bytes=vmem)


def _rms(v):
    return v * lax.rsqrt(jnp.mean(v * v, axis=-1, keepdims=True) + EPS)


def _modulated(x, gain, scale, shift):
    return _rms(x) * (gain * (1.0 + scale)) + shift


def _gelu(v):
    return 0.5 * v * (1.0 + lax.erf(v * (1.0 / math.sqrt(2.0))))


def _ada_kernel(c_ref, w_ref, b_ref, o_ref):
    c = c_ref[...]
    ca = (c * jax.nn.sigmoid(c)).astype(BF16)
    o_ref[...] = jnp.dot(ca, w_ref[...].astype(BF16), preferred_element_type=F32) + b_ref[...]


def _ada(c, w, b):
    bsz, d = c.shape
    n = w.shape[1]
    rows = F32_SUBLANES
    assert bsz <= rows and n % ADA_TN == 0
    cp = jnp.zeros((rows, d), F32).at[:bsz].set(c)
    tn = ADA_TN
    out = pl.pallas_call(
        _ada_kernel,
        grid=(n // tn,),
        in_specs=[pl.BlockSpec((rows, d), lambda j: (0, 0)),
                  pl.BlockSpec((d, tn), lambda j: (0, j)),
                  pl.BlockSpec((1, tn), lambda j: (0, j))],
        out_specs=pl.BlockSpec((rows, tn), lambda j: (0, j)),
        out_shape=jax.ShapeDtypeStruct((rows, n), F32),
        compiler_params=_params(("arbitrary",)),
        name="adaln",
    )(cp, w, b.reshape(1, n))
    return out[:bsz]


def _pack_gu_rows(src_ref, dst_ref):
    n_f, _, two_tf = dst_ref.shape
    tf = two_tf // 2
    d_ff = src_ref.shape[1] // 2
    for f in range(n_f):
        w = min(tf, d_ff - f * tf)
        dst_ref[f, :, :w] = src_ref[:, f * tf:f * tf + w].astype(BF16)
        dst_ref[f, :, w:2 * w] = src_ref[:, d_ff + f * tf:d_ff + f * tf + w].astype(BF16)
        if w < tf:
            dst_ref[f, :, 2 * w:] = jnp.zeros((dst_ref.shape[1], two_tf - 2 * w), BF16)


def _pack_gu_kernel(src_ref, dst_ref):
    _pack_gu_rows(src_ref, dst_ref)


def _pack_gu(w_gu, tf):
    d, two_f = w_gu.shape
    n_f = pl.cdiv(two_f // 2, tf)
    rb = PACK_ROWS
    return pl.pallas_call(
        _pack_gu_kernel,
        grid=(d // rb,),
        in_specs=[pl.BlockSpec((rb, two_f), lambda r: (r, 0))],
        out_specs=pl.BlockSpec((n_f, rb, 2 * tf), lambda r: (0, r, 0)),
        out_shape=jax.ShapeDtypeStruct((n_f, d, 2 * tf), BF16),
        compiler_params=_params(("parallel",)),
        name="pack_gu",
    )(w_gu)


def _ffn_kernel(*refs, last_width, job_kinds):
    n_job = len(job_kinds)
    x_ref, sh_ref, sc_ref, gt_ref, pre_ref, post_ref, wgu_ref, wd_ref = refs[:8]
    job_src = refs[8:8 + n_job]
    o_ref = refs[8 + n_job]
    job_dst = refs[9 + n_job:9 + 2 * n_job]
    (h_ref,) = refs[9 + 2 * n_job:]
    acc_ref = o_ref
    step = pl.program_id(1)
    n_f = pl.num_programs(1)
    tf = wd_ref.shape[0]
    is_partial = _ffn_tile(pl.program_id(0), step, n_f) == n_f - 1

    def side_jobs():
        for kind, src, dst in zip(job_kinds, job_src, job_dst):
            if kind == "gu":
                _pack_gu_rows(src, dst)
            else:
                dst[...] = src[...].astype(BF16)

    def down_input(h, width):
        gu = jnp.dot(h, wgu_ref[0, :, :2 * width], preferred_element_type=F32)
        g, u = gu[:, :width], gu[:, width:]
        return (g * jax.nn.sigmoid(g) * u).astype(BF16)

    tm = x_ref.shape[0]
    halves = (slice(0, tm // 2), slice(tm // 2, tm))

    def accumulate(width):
        for r in halves:
            a = down_input(h_ref[r, :], width)
            acc_ref[r, :] += jnp.dot(a, wd_ref[:width, :], preferred_element_type=F32)

    @pl.when(step == 0)
    def _():
        for r in halves:
            h = _modulated(x_ref[r, :], pre_ref[...], sc_ref[0], sh_ref[0]).astype(BF16)
            h_ref[r, :] = h
            acc_ref[r, :] = jnp.dot(down_input(h, tf), wd_ref[...], preferred_element_type=F32)
        side_jobs()

    @pl.when((step > 0) & (step < n_f - 1) & jnp.logical_not(is_partial))
    def _():
        accumulate(tf)
        side_jobs()

    @pl.when(is_partial)
    def _():
        accumulate(last_width)
        side_jobs()

    @pl.when(step == n_f - 1)
    def _():
        for r in halves:
            a = down_input(h_ref[r, :], tf)
            y = acc_ref[r, :] + jnp.dot(a, wd_ref[...], preferred_element_type=F32)
            o_ref[r, :] = x_ref[r, :] + (0.5 * gt_ref[0] * post_ref[...]) * _rms(y)
        side_jobs()


def _ffn_tile(i, step, n_f):
    mid = (n_f - 1) // 2
    pos = jnp.where(i % 2 == 0, step, n_f - 1 - step)
    return jnp.where(pos == mid, n_f - 1, jnp.where(pos < mid, pos, pos - 1))


def _job_rows(n_rows, n_steps):
    rb = BF16_SUBLANES
    while n_rows % rb or n_rows // rb > n_steps:
        rb += BF16_SUBLANES
    return rb


def _ffn(x2, seq, sh, sc, gt, pre_g, post_g, wgu, wd, jobs=()):
    t, d = x2.shape
    d_ff = wd.shape[0]
    tm, tf = min(FFN_TM, seq), wgu.shape[2] // 2
    n_f = wgu.shape[0]
    assert n_f >= 3 and n_f == pl.cdiv(d_ff, tf)
    n_steps = (t // tm) * n_f
    per_b = seq // tm
    mod_spec = pl.BlockSpec((1, 1, d), lambda i, f: (i // per_b, 0, 0))
    vec_spec = pl.BlockSpec((1, d), lambda i, f: (0, 0))

    job_in, job_out, job_shape = [], [], []
    for kind, w in jobs:
        rows, cols = w.shape
        rb = _job_rows(rows, n_steps)
        blk = lambda i, f, n=rows // rb: jnp.minimum(i * n_f + f, n - 1)
        job_in.append(pl.BlockSpec((rb, cols), lambda i, f, blk=blk: (blk(i, f), 0)))
        if kind == "gu":
            job_out.append(pl.BlockSpec((n_f, rb, 2 * tf), lambda i, f, blk=blk: (0, blk(i, f), 0)))
            job_shape.append(jax.ShapeDtypeStruct((n_f, rows, 2 * tf), BF16))
        else:
            job_out.append(pl.BlockSpec((rb, cols), lambda i, f, blk=blk: (blk(i, f), 0)))
            job_shape.append(jax.ShapeDtypeStruct((rows, cols), BF16))

    out = pl.pallas_call(
        functools.partial(_ffn_kernel, last_width=d_ff - (n_f - 1) * tf,
                          job_kinds=tuple(kind for kind, _ in jobs)),
        grid=(t // tm, n_f),
        in_specs=[pl.BlockSpec((tm, d), lambda i, f: (i, 0)),
                  mod_spec, mod_spec, mod_spec, vec_spec, vec_spec,
                  pl.BlockSpec((1, d, 2 * tf), lambda i, f: (_ffn_tile(i, f, n_f), 0, 0)),
                  pl.BlockSpec((tf, d), lambda i, f: (_ffn_tile(i, f, n_f), 0))] + job_in,
        out_specs=[pl.BlockSpec((tm, d), lambda i, f: (i, 0))] + job_out,
        out_shape=[jax.ShapeDtypeStruct((t, d), F32)] + job_shape,
        scratch_shapes=[pltpu.VMEM((tm, d), BF16)],
        compiler_params=_params(("arbitrary", "arbitrary")),
        name="ffn_convert" if jobs else "ffn",
    )(x2, sh, sc, gt, pre_g, post_g, wgu, wd, *[w for _, w in jobs])
    return out[0], out[1:]


def _proj_kernel(x_ref, sh_ref, sc_ref, pre_ref, lng_ref, lnb_ref, w_ref, o_ref):
    n_slab = o_ref.shape[0] // PROJ_SECTIONS
    tn = n_slab * LANES
    h = _modulated(x_ref[...], pre_ref[...], sc_ref[0], sh_ref[0]).astype(BF16)

    def emit(j, epilogue):
        r = jnp.dot(h, w_ref[:, j * tn:(j + 1) * tn], preferred_element_type=F32)
        val = epilogue(r).astype(BF16)
        for p in range(n_slab):
            o_ref[j * n_slab + p] = val[:, p * LANES:(p + 1) * LANES]

    def gelu_layer_norm(v):
        ge = _gelu(v)
        xc = ge - jnp.mean(ge, axis=-1, keepdims=True)
        y = xc * lax.rsqrt(jnp.mean(xc * xc, axis=-1, keepdims=True) + EPS)
        return y * lng_ref[...] + lnb_ref[...]

    emit(4, gelu_layer_norm)
    emit(1, lambda v: v)
    emit(3, _gelu)
    emit(0, lambda v: v * (LOG2E / math.sqrt(HEAD_DIM)))
    emit(2, lambda v: v)


def _proj(x2, seq, sh, sc, pre_g, ln_g, ln_b, w_in):
    t, d = x2.shape
    n = w_in.shape[1]
    tn = n // PROJ_SECTIONS
    n_slab = tn // LANES
    tm = min(PROJ_TM, seq)
    per_b = seq // tm
    mod_spec = pl.BlockSpec((1, 1, d), lambda i: (i // per_b, 0, 0))
    vec_spec = pl.BlockSpec((1, d), lambda i: (0, 0))
    ln_spec = pl.BlockSpec((1, tn), lambda i: (0, 0))
    return pl.pallas_call(
        _proj_kernel,
        grid=(t // tm,),
        in_specs=[pl.BlockSpec((tm, d), lambda i: (i, 0)),
                  mod_spec, mod_spec, vec_spec, ln_spec, ln_spec,
                  pl.BlockSpec((d, n), lambda i: (0, 0), pipeline_mode=pl.Buffered(1))],
        out_specs=pl.BlockSpec((PROJ_SECTIONS * n_slab, tm, LANES), lambda i: (0, i, 0)),
        out_shape=jax.ShapeDtypeStruct((PROJ_SECTIONS * n_slab, t, LANES), BF16),
        compiler_params=_params(("parallel",)),
        name="in_proj",
    )(x2, sh, sc, pre_g, ln_g, ln_b, w_in)


def _attn_kernel(q_ref, kp_ref, kc_ref, vp_ref, vc_ref, rel_ref, o_ref, tab_ref):
    n_sub = ATT_QB // ATT_SUB
    n_slab = q_ref.shape[0]
    n_head = tab_ref.shape[1]

    @pl.when((pl.program_id(1) == 0) & (pl.program_id(2) == 0))
    def _():
        r = lax.broadcasted_iota(jnp.int32, (ATT_SUB, ATT_KW), 0)
        c = lax.broadcasted_iota(jnp.int32, (ATT_SUB, ATT_KW), 1)
        lo = (r // CHUNK) * CHUNK
        in_band = (c >= lo) & (c < lo + (LEFT_CHUNKS + 1) * CHUNK)
        for h in range(n_head):
            row = (rel_ref[0, h:h + 1, :] - rel_ref[0, h:h + 1, 0:1]) * LOG2E
            base = jnp.broadcast_to(row, (ATT_SUB, REL_ROW))
            toe = pltpu.roll(base, 0, 1, stride=1, stride_axis=0)[:, :ATT_KW]
            t0 = jnp.where(in_band, toe, NEG)
            tab_ref[0, h] = t0
            for i in range(n_sub):
                tab_ref[1 + i, h] = jnp.where(c + i * ATT_SUB >= ATT_QB, t0, NEG)

    qblk = pl.program_id(2)

    pair = LANES // HEAD_DIM
    upper = lax.broadcasted_iota(jnp.int32, (ATT_SUB, LANES), 1) >= HEAD_DIM
    ahead = 2
    near = ATT_KW - 2 * MAX_REL

    def attend_slab(slab, sequence_start):
        q = q_ref[slab]
        k = jnp.concatenate([kp_ref[slab], kc_ref[slab]], axis=0)
        v = jnp.concatenate([vp_ref[slab], vc_ref[slab]], axis=0)
        heads = slice(slab * pair, (slab + 1) * pair)

        def scores(i):
            qs = q[i * ATT_SUB:(i + 1) * ATT_SUB].astype(F32)
            qh = jnp.concatenate([jnp.where(upper, 0.0, qs), jnp.where(upper, qs, 0.0)], axis=0)
            ks = k[i * ATT_SUB:i * ATT_SUB + ATT_KW]
            return lax.dot_general(qh.astype(BF16), ks, (((1,), (1,)), ((), ())), preferred_element_type=F32)

        pending = [scores(i) for i in range(ahead)]
        for i in range(n_sub):
            r0 = i * ATT_SUB
            vs = v[r0:r0 + ATT_KW]
            s = pending.pop(0).reshape(pair, ATT_SUB, ATT_KW)
            if i + ahead < n_sub:
                pending.append(scores(i + ahead))
            if sequence_start:
                s = s + tab_ref[1 + i, heads]
            else:
                s = jnp.concatenate([s[..., :LANES] + tab_ref[0, heads, :, :LANES], s[..., LANES:near],
                                     s[..., near:] + tab_ref[0, heads, :, near:]], axis=-1)
            m = jnp.max(s, axis=-1, keepdims=True)
            p = jnp.exp2(s - m)
            l = jnp.sum(p, axis=-1, keepdims=True)
            pv = jnp.dot(p.reshape(pair * ATT_SUB, ATT_KW).astype(BF16), vs, preferred_element_type=F32)
            pv = pv.reshape(pair, ATT_SUB, LANES) / l
            o_ref[slab, r0:r0 + ATT_SUB, :] = jnp.where(upper, pv[1], pv[0]).astype(o_ref.dtype)

    def attend(sequence_start):
        for slab in range(n_slab):
            attend_slab(slab, sequence_start)

    @pl.when(qblk == 0)
    def _():
        attend(True)

    @pl.when(qblk > 0)
    def _():
        attend(False)


def _attn(slabs, rel_rows, bsz, seq):
    n_grp, n_head = rel_rows.shape[:2]
    n_slab = n_head * HEAD_DIM // LANES
    t = slabs.shape[1]
    nq = seq // ATT_QB
    blk = (n_slab, ATT_QB, LANES)

    def cur(section):
        return lambda p, b, i: (section * n_grp + p, b * nq + i, 0)

    def prev(section):
        return lambda p, b, i: (section * n_grp + p, b * nq + jnp.maximum(i - 1, 0), 0)

    return pl.pallas_call(
        _attn_kernel,
        grid=(n_grp, bsz, nq),
        in_specs=[pl.BlockSpec(blk, cur(0)),
                  pl.BlockSpec(blk, prev(1)), pl.BlockSpec(blk, cur(1)),
                  pl.BlockSpec(blk, prev(2)), pl.BlockSpec(blk, cur(2)),
                  pl.BlockSpec((1, n_head, REL_ROW), lambda p, b, i: (p, 0, 0))],
        out_specs=pl.BlockSpec(blk, lambda p, b, i: (p, b * nq + i, 0)),
        out_shape=jax.ShapeDtypeStruct((n_grp * n_slab, t, LANES), BF16),
        scratch_shapes=[pltpu.VMEM((1 + ATT_QB // ATT_SUB, n_head, ATT_SUB, ATT_KW), F32)],
        compiler_params=_params(("arbitrary", "arbitrary", "arbitrary")),
        name="chunk_attn",
    )(slabs, slabs, slabs, slabs, slabs, rel_rows)


def _rel_rows(rel_bias):
    n_head = rel_bias.shape[0]
    rb = rel_bias.astype(F32)
    far = rb[:, 2 * MAX_REL:]
    n_far = ATT_KW - 2 * MAX_REL
    row = jnp.concatenate([jnp.broadcast_to(far, (n_head, n_far)), rb[:, :0:-1],
                           jnp.broadcast_to(far, (n_head, REL_ROW - ATT_KW))], axis=1)
    return row.reshape(n_head // ATT_HEADS, ATT_HEADS, REL_ROW)


def _gate_kernel(u_ref, v_ref, w_ref, b_ref, o_ref):
    n_grp = u_ref.shape[0]
    tm = u_ref.shape[1]
    ti = lax.broadcasted_iota(jnp.int32, (GATE_BLOCK, GATE_BLOCK), 0)
    si = lax.broadcasted_iota(jnp.int32, (GATE_BLOCK, GATE_BLOCK), 1)
    causal = (ti // CHUNK) >= (si // CHUNK)
    blocks = [slice(n * GATE_BLOCK, (n + 1) * GATE_BLOCK) for n in range(tm // GATE_BLOCK)]
    for g in range(n_grp):
        w = jnp.where(causal, w_ref[g], 0.0).astype(BF16)
        v_wide = jnp.concatenate([v_ref[g, rows, :] for rows in blocks], axis=1)
        z = jnp.dot(w, v_wide, preferred_element_type=F32)
        for n, rows in enumerate(blocks):
            zn = z[:, n * LANES:(n + 1) * LANES] + b_ref[g]
            o_ref[g, rows, :] = (u_ref[g, rows, :].astype(F32) * zn).astype(o_ref.dtype)


def _gate(slabs, w_s, b_s):
    n_grp = w_s.shape[0]
    t = slabs.shape[1]
    tm = GATE_TM
    b_bc = jnp.broadcast_to(b_s.astype(F32)[:, :, None], (n_grp, GATE_BLOCK, LANES))
    return pl.pallas_call(
        _gate_kernel,
        grid=(t // tm,),
        in_specs=[pl.BlockSpec((n_grp, tm, LANES), lambda i: (3, i, 0)),
                  pl.BlockSpec((n_grp, tm, LANES), lambda i: (4, i, 0)),
                  pl.BlockSpec((n_grp, GATE_BLOCK, GATE_BLOCK), lambda i: (0, 0, 0)),
                  pl.BlockSpec((n_grp, GATE_BLOCK, LANES), lambda i: (0, 0, 0))],
        out_specs=pl.BlockSpec((n_grp, tm, LANES), lambda i: (0, i, 0)),
        out_shape=jax.ShapeDtypeStruct((n_grp, t, LANES), BF16),
        compiler_params=_params(("parallel",)),
        name="spatial_gate",
    )(slabs, slabs, w_s, b_bc)


def _out_kernel(x_ref, a_ref, b_ref, ga_ref, gb_ref, gt_ref, post_ref, w_ref, o_ref):
    n_slab = a_ref.shape[0]
    width = n_slab * LANES
    tm = x_ref.shape[0]

    def normed_slabs(src_ref, g_ref, r):
        ss = None
        for p in range(n_slab):
            v = src_ref[p, r, :].astype(F32)
            part = jnp.sum(v * v, axis=-1, keepdims=True)
            ss = part if ss is None else ss + part
        inv = lax.rsqrt(ss * (1.0 / width) + EPS)
        return [(src_ref[p, r, :].astype(F32) * inv * g_ref[:, p * LANES:(p + 1) * LANES]).astype(BF16)
                for p in range(n_slab)]

    coef = gt_ref[0] * post_ref[...]
    for r in (slice(0, tm // 2), slice(tm // 2, tm)):
        merged = jnp.concatenate(normed_slabs(a_ref, ga_ref, r) + normed_slabs(b_ref, gb_ref, r), axis=1)
        y = jnp.dot(merged, w_ref[...], preferred_element_type=F32)
        o_ref[r, :] = x_ref[r, :] + coef * _rms(y)


def _out_proj(x2, seq, a_slabs, b_slabs, g_a, g_b, gt, post_g, w_out):
    t, d = x2.shape
    n_slab = a_slabs.shape[0]
    tm = min(OUT_TM, seq)
    per_b = seq // tm
    slab_spec = pl.BlockSpec((n_slab, tm, LANES), lambda i: (0, i, 0))
    half_spec = pl.BlockSpec((1, n_slab * LANES), lambda i: (0, 0))
    return pl.pallas_call(
        _out_kernel,
        grid=(t // tm,),
        in_specs=[pl.BlockSpec((tm, d), lambda i: (i, 0)),
                  slab_spec, slab_spec, half_spec, half_spec,
                  pl.BlockSpec((1, 1, d), lambda i: (i // per_b, 0, 0)),
                  pl.BlockSpec((1, d), lambda i: (0, 0)),
                  pl.BlockSpec((d, d), lambda i: (0, 0), pipeline_mode=pl.Buffered(1))],
        out_specs=pl.BlockSpec((tm, d), lambda i: (i, 0)),
        out_shape=jax.ShapeDtypeStruct((t, d), F32),
        compiler_params=_params(("parallel",)),
        name="out_proj",
    )(x2, a_slabs, b_slabs, g_a, g_b, gt, post_g, w_out)


def kernel(x, c, w_ada, b_ada, ffn1_pre_g, ffn1_post_g, ffn1_w_gu, ffn1_w_down, mix_pre_g, mix_post_g, w_in, rel_bias, ln_v_g, ln_v_b, w_s, b_s, g_out_a, g_out_b, w_out, ffn2_pre_g, ffn2_post_g, ffn2_w_gu, ffn2_w_down):
    bsz, seq, d = x.shape
    depth = w_ada.shape[0]
    assert seq % ATT_QB == 0 and seq % min(PROJ_TM, seq) == 0
    x2 = x.reshape(bsz * seq, d)
    row = lambda v: v.reshape(1, -1)
    for l in range(depth):
        mod = _ada(c, w_ada[l], b_ada[l]).reshape(bsz, N_MOD, 1, d)
        sh1, sc1, gt1, sh2, sc2, gt2, sh3, sc3, gt3 = [mod[:, i] for i in range(N_MOD)]

        jobs = (("gu", ffn2_w_gu[l]), ("cast", ffn2_w_down[l]), ("cast", w_in[l]), ("cast", w_out[l]))
        x2, (wgu2, wd2, w_in_bf, w_out_bf) = _ffn(
            x2, seq, sh1, sc1, gt1, row(ffn1_pre_g[l]), row(ffn1_post_g[l]),
            _pack_gu(ffn1_w_gu[l], FFN_TF), ffn1_w_down[l].astype(BF16), jobs)

        slabs = _proj(x2, seq, sh2, sc2, row(mix_pre_g[l]), row(ln_v_g[l]), row(ln_v_b[l]), w_in_bf)
        out_a = _attn(slabs, _rel_rows(rel_bias[l]), bsz, seq)
        out_b = _gate(slabs, w_s[l], b_s[l])
        x2 = _out_proj(x2, seq, out_a, out_b, row(g_out_a[l]), row(g_out_b[l]), gt2,
                       row(mix_post_g[l]), w_out_bf)

        x2, _ = _ffn(x2, seq, sh3, sc3, gt3, row(ffn2_pre_g[l]), row(ffn2_post_g[l]), wgu2, wd2)
    return x2.reshape(bsz, seq, d)
```

```python
import functools
import math

import jax
import jax.numpy as jnp
from jax import lax
from jax.experimental import pallas as pl
from jax.experimental.pallas import tpu as pltpu

F32 = jnp.float32
BF16 = jnp.bfloat16

EPS = 1e-6
CHUNK = 64
LEFT_CHUNKS = 8
HEAD_DIM = 64
MAX_REL = 128
GATE_BLOCK = 128
N_MOD = 9

LANES = 128
BF16_SUBLANES = 16
F32_SUBLANES = 8
PACK_ROWS = 128
ADA_TN = 1024
VMEM_LIMIT = 58 * 1024 * 1024

FFN_TM = 1024
FFN_TF = 512
PROJ_TM = 512
PROJ_SECTIONS = 5
ATT_QB = 512
ATT_SUB = 128
ATT_HEADS = 16
ATT_KW = ATT_SUB + LEFT_CHUNKS * CHUNK
REL_ROW = ATT_KW + ATT_SUB
OUT_TM = 512
NEG = -1e30
LOG2E = math.log2(math.e)


def _params(sem, vmem=VMEM_LIMIT):
    return pltpu.CompilerParams(dimension_semantics=sem, vmem_limit_bytes=vmem)


def _rms(v):
    return v * lax.rsqrt(jnp.mean(v * v, axis=-1, keepdims=True) + EPS)


def _modulated(x, gain, scale, shift):
    return _rms(x) * (gain * (1.0 + scale)) + shift


def _gelu(v):
    return 0.5 * v * (1.0 + lax.erf(v * (1.0 / math.sqrt(2.0))))


def _ada_kernel(c_ref, w_ref, b_ref, o_ref):
    c = c_ref[...]
    ca = (c * jax.nn.sigmoid(c)).astype(BF16)
    o_ref[...] = jnp.dot(ca, w_ref[...].astype(BF16), preferred_element_type=F32) + b_ref[...]


def _ada(c, w, b):
    bsz, d = c.shape
    n = w.shape[1]
    rows = F32_SUBLANES
    assert bsz <= rows and n % ADA_TN == 0
    cp = jnp.zeros((rows, d), F32).at[:bsz].set(c)
    tn = ADA_TN
    out = pl.pallas_call(
        _ada_kernel,
        grid=(n // tn,),
        in_specs=[pl.BlockSpec((rows, d), lambda j: (0, 0)),
                  pl.BlockSpec((d, tn), lambda j: (0, j)),
                  pl.BlockSpec((1, tn), lambda j: (0, j))],
        out_specs=pl.BlockSpec((rows, tn), lambda j: (0, j)),
        out_shape=jax.ShapeDtypeStruct((rows, n), F32),
        compiler_params=_params(("arbitrary",)),
        name="adaln",
    )(cp, w, b.reshape(1, n))
    return out[:bsz]


def _pack_gu_rows(src_ref, dst_ref):
    n_f, _, two_tf = dst_ref.shape
    tf = two_tf // 2
    d_ff = src_ref.shape[1] // 2
    for f in range(n_f):
        w = min(tf, d_ff - f * tf)
        dst_ref[f, :, :w] = src_ref[:, f * tf:f * tf + w].astype(BF16)
        dst_ref[f, :, w:2 * w] = src_ref[:, d_ff + f * tf:d_ff + f * tf + w].astype(BF16)
        if w < tf:
            dst_ref[f, :, 2 * w:] = jnp.zeros((dst_ref.shape[1], two_tf - 2 * w), BF16)


def _pack_gu(w_gu, tf):
    d, two_f = w_gu.shape
    n_f = pl.cdiv(two_f // 2, tf)
    rb = PACK_ROWS
    return pl.pallas_call(
        _pack_gu_rows,
        grid=(d // rb,),
        in_specs=[pl.BlockSpec((rb, two_f), lambda r: (r, 0))],
        out_specs=pl.BlockSpec((n_f, rb, 2 * tf), lambda r: (0, r, 0)),
        out_shape=jax.ShapeDtypeStruct((n_f, d, 2 * tf), BF16),
        compiler_params=_params(("parallel",)),
        name="pack_gu",
    )(w_gu)


def _ffn_kernel(*refs, last_width, job_kinds):
    n_job = len(job_kinds)
    x_ref, sh_ref, sc_ref, gt_ref, pre_ref, post_ref, wgu_ref, wd_ref = refs[:8]
    job_src = refs[8:8 + n_job]
    o_ref = refs[8 + n_job]
    job_dst = refs[9 + n_job:9 + 2 * n_job]
    (h_ref,) = refs[9 + 2 * n_job:]
    acc_ref = o_ref
    step = pl.program_id(1)
    n_f = pl.num_programs(1)
    tf = wd_ref.shape[0]
    is_partial = _ffn_tile(pl.program_id(0), step, n_f) == n_f - 1

    def side_jobs():
        for kind, src, dst in zip(job_kinds, job_src, job_dst):
            if kind == "gu":
                _pack_gu_rows(src, dst)
            else:
                dst[...] = src[...].astype(BF16)

    def down_input(h, width):
        gu = jnp.dot(h, wgu_ref[0, :, :2 * width], preferred_element_type=F32)
        g, u = gu[:, :width], gu[:, width:]
        return (g * jax.nn.sigmoid(g) * u).astype(BF16)

    tm = x_ref.shape[0]
    halves = (slice(0, tm // 2), slice(tm // 2, tm))

    def accumulate(width):
        for r in halves:
            a = down_input(h_ref[r, :], width)
            acc_ref[r, :] += jnp.dot(a, wd_ref[:width, :], preferred_element_type=F32)

    @pl.when(step == 0)
    def _():
        for r in halves:
            h = _modulated(x_ref[r, :], pre_ref[...], sc_ref[0], sh_ref[0]).astype(BF16)
            h_ref[r, :] = h
            acc_ref[r, :] = jnp.dot(down_input(h, tf), wd_ref[...], preferred_element_type=F32)
        side_jobs()

    @pl.when((step > 0) & (step < n_f - 1) & jnp.logical_not(is_partial))
    def _():
        accumulate(tf)
        side_jobs()

    @pl.when(is_partial)
    def _():
        accumulate(last_width)
        side_jobs()

    @pl.when(step == n_f - 1)
    def _():
        for r in halves:
            a = down_input(h_ref[r, :], tf)
            y = acc_ref[r, :] + jnp.dot(a, wd_ref[...], preferred_element_type=F32)
            o_ref[r, :] = x_ref[r, :] + (0.5 * gt_ref[0] * post_ref[...]) * _rms(y)
        side_jobs()


def _ffn_tile(i, step, n_f):
    mid = (n_f - 1) // 2
    pos = jnp.where(i % 2 == 0, step, n_f - 1 - step)
    return jnp.where(pos == mid, n_f - 1, jnp.where(pos < mid, pos, pos - 1))


def _job_rows(n_rows, n_steps):
    rb = BF16_SUBLANES
    while n_rows % rb or n_rows // rb > n_steps:
        rb += BF16_SUBLANES
    return rb


def _ffn(x2, seq, sh, sc, gt, pre_g, post_g, wgu, wd, jobs=()):
    t, d = x2.shape
    d_ff = wd.shape[0]
    tm, tf = min(FFN_TM, seq), wgu.shape[2] // 2
    n_f = wgu.shape[0]
    assert n_f >= 3 and n_f == pl.cdiv(d_ff, tf)
    n_steps = (t // tm) * n_f
    per_b = seq // tm
    mod_spec = pl.BlockSpec((1, 1, d), lambda i, f: (i // per_b, 0, 0))
    vec_spec = pl.BlockSpec((1, d), lambda i, f: (0, 0))

    job_in, job_out, job_shape = [], [], []
    for kind, w in jobs:
        rows, cols = w.shape
        rb = _job_rows(rows, n_steps)
        blk = lambda i, f, n=rows // rb: jnp.minimum(i * n_f + f, n - 1)
        job_in.append(pl.BlockSpec((rb, cols), lambda i, f, blk=blk: (blk(i, f), 0)))
        if kind == "gu":
            job_out.append(pl.BlockSpec((n_f, rb, 2 * tf), lambda i, f, blk=blk: (0, blk(i, f), 0)))
            job_shape.append(jax.ShapeDtypeStruct((n_f, rows, 2 * tf), BF16))
        else:
            job_out.append(pl.BlockSpec((rb, cols), lambda i, f, blk=blk: (blk(i, f), 0)))
            job_shape.append(jax.ShapeDtypeStruct((rows, cols), BF16))

    out = pl.pallas_call(
        functools.partial(_ffn_kernel, last_width=d_ff - (n_f - 1) * tf,
                          job_kinds=tuple(kind for kind, _ in jobs)),
        grid=(t // tm, n_f),
        in_specs=[pl.BlockSpec((tm, d), lambda i, f: (i, 0)),
                  mod_spec, mod_spec, mod_spec, vec_spec, vec_spec,
                  pl.BlockSpec((1, d, 2 * tf), lambda i, f: (_ffn_tile(i, f, n_f), 0, 0)),
                  pl.BlockSpec((tf, d), lambda i, f: (_ffn_tile(i, f, n_f), 0))] + job_in,
        out_specs=[pl.BlockSpec((tm, d), lambda i, f: (i, 0))] + job_out,
        out_shape=[jax.ShapeDtypeStruct((t, d), F32)] + job_shape,
        scratch_shapes=[pltpu.VMEM((tm, d), BF16)],
        compiler_params=_params(("arbitrary", "arbitrary")),
        name="ffn_convert" if jobs else "ffn",
    )(x2, sh, sc, gt, pre_g, post_g, wgu, wd, *[w for _, w in jobs])
    return out[0], out[1:]


def _proj_kernel(x_ref, sh_ref, sc_ref, pre_ref, lng_ref, lnb_ref, w_ref, o_ref):
    n_slab = o_ref.shape[0] // PROJ_SECTIONS
    tn = n_slab * LANES
    h = _modulated(x_ref[...], pre_ref[...], sc_ref[0], sh_ref[0]).astype(BF16)

    def emit(j, epilogue):
        r = jnp.dot(h, w_ref[:, j * tn:(j + 1) * tn], preferred_element_type=F32)
        val = epilogue(r).astype(BF16)
        for p in range(n_slab):
            o_ref[j * n_slab + p] = val[:, p * LANES:(p + 1) * LANES]

    def gelu_layer_norm(v):
        ge = _gelu(v)
        xc = ge - jnp.mean(ge, axis=-1, keepdims=True)
        y = xc * lax.rsqrt(jnp.mean(xc * xc, axis=-1, keepdims=True) + EPS)
        return y * lng_ref[...] + lnb_ref[...]

    emit(4, gelu_layer_norm)
    emit(1, lambda v: v)
    emit(3, _gelu)
    emit(0, lambda v: v * (LOG2E / math.sqrt(HEAD_DIM)))
    emit(2, lambda v: v)


def _proj(x2, seq, sh, sc, pre_g, ln_g, ln_b, w_in):
    t, d = x2.shape
    n = w_in.shape[1]
    tn = n // PROJ_SECTIONS
    n_slab = tn // LANES
    tm = min(PROJ_TM, seq)
    per_b = seq // tm
    mod_spec = pl.BlockSpec((1, 1, d), lambda i: (i // per_b, 0, 0))
    vec_spec = pl.BlockSpec((1, d), lambda i: (0, 0))
    ln_spec = pl.BlockSpec((1, tn), lambda i: (0, 0))
    return pl.pallas_call(
        _proj_kernel,
        grid=(t // tm,),
        in_specs=[pl.BlockSpec((tm, d), lambda i: (i, 0)),
                  mod_spec, mod_spec, vec_spec, ln_spec, ln_spec,
                  pl.BlockSpec((d, n), lambda i: (0, 0), pipeline_mode=pl.Buffered(1))],
        out_specs=pl.BlockSpec((PROJ_SECTIONS * n_slab, tm, LANES), lambda i: (0, i, 0)),
        out_shape=jax.ShapeDtypeStruct((PROJ_SECTIONS * n_slab, t, LANES), BF16),
        compiler_params=_params(("parallel",)),
        name="in_proj",
    )(x2, sh, sc, pre_g, ln_g, ln_b, w_in)


def _attn_kernel(q_ref, kp_ref, kc_ref, vp_ref, vc_ref, rel_ref, o_ref, tab_ref):
    n_sub = ATT_QB // ATT_SUB
    n_slab = q_ref.shape[0]
    n_head = tab_ref.shape[1]

    @pl.when((pl.program_id(1) == 0) & (pl.program_id(2) == 0))
    def _():
        r = lax.broadcasted_iota(jnp.int32, (ATT_SUB, ATT_KW), 0)
        c = lax.broadcasted_iota(jnp.int32, (ATT_SUB, ATT_KW), 1)
        lo = (r // CHUNK) * CHUNK
        in_band = (c >= lo) & (c < lo + (LEFT_CHUNKS + 1) * CHUNK)
        for h in range(n_head):
            row = (rel_ref[0, h:h + 1, :] - rel_ref[0, h:h + 1, 0:1]) * LOG2E
            base = jnp.broadcast_to(row, (ATT_SUB, REL_ROW))
            toe = pltpu.roll(base, 0, 1, stride=1, stride_axis=0)[:, :ATT_KW]
            t0 = jnp.where(in_band, toe, NEG)
            tab_ref[0, h] = t0
            for i in range(n_sub):
                tab_ref[1 + i, h] = jnp.where(c + i * ATT_SUB >= ATT_QB, t0, NEG)

    qblk = pl.program_id(2)

    pair = LANES // HEAD_DIM
    upper = lax.broadcasted_iota(jnp.int32, (ATT_SUB, LANES), 1) >= HEAD_DIM
    ahead = 2
    near = ATT_KW - 2 * MAX_REL

    def attend_slab(slab, sequence_start):
        q = q_ref[slab]
        k = jnp.concatenate([kp_ref[slab], kc_ref[slab]], axis=0)
        v = jnp.concatenate([vp_ref[slab], vc_ref[slab]], axis=0)
        heads = slice(slab * pair, (slab + 1) * pair)

        def scores(i):
            qs = q[i * ATT_SUB:(i + 1) * ATT_SUB].astype(F32)
            qh = jnp.concatenate([jnp.where(upper, 0.0, qs), jnp.where(upper, qs, 0.0)], axis=0)
            ks = k[i * ATT_SUB:i * ATT_SUB + ATT_KW]
            return lax.dot_general(qh.astype(BF16), ks, (((1,), (1,)), ((), ())), preferred_element_type=F32)

        pending = [scores(i) for i in range(ahead)]
        for i in range(n_sub):
            r0 = i * ATT_SUB
            vs = v[r0:r0 + ATT_KW]
            s = pending.pop(0).reshape(pair, ATT_SUB, ATT_KW)
            if i + ahead < n_sub:
                pending.append(scores(i + ahead))
            if sequence_start:
                s = s + tab_ref[1 + i, heads]
            else:
                s = jnp.concatenate([s[..., :LANES] + tab_ref[0, heads, :, :LANES], s[..., LANES:near],
                                     s[..., near:] + tab_ref[0, heads, :, near:]], axis=-1)
            m = jnp.max(s, axis=-1, keepdims=True)
            p = jnp.exp2(s - m)
            l = jnp.sum(p, axis=-1, keepdims=True)
            pv = jnp.dot(p.reshape(pair * ATT_SUB, ATT_KW).astype(BF16), vs, preferred_element_type=F32)
            pv = pv.reshape(pair, ATT_SUB, LANES) / l
            o_ref[slab, r0:r0 + ATT_SUB, :] = jnp.where(upper, pv[1], pv[0]).astype(o_ref.dtype)

    def attend(sequence_start):
        for slab in range(n_slab):
            attend_slab(slab, sequence_start)

    @pl.when(qblk == 0)
    def _():
        attend(True)

    @pl.when(qblk > 0)
    def _():
        attend(False)


def _attn(slabs, rel_rows, bsz, seq):
    n_grp, n_head = rel_rows.shape[:2]
    n_slab = n_head * HEAD_DIM // LANES
    t = slabs.shape[1]
    nq = seq // ATT_QB
    blk = (n_slab, ATT_QB, LANES)

    def cur(section):
        return lambda p, b, i: (section * n_grp + p, b * nq + i, 0)

    def prev(section):
        return lambda p, b, i: (section * n_grp + p, b * nq + jnp.maximum(i - 1, 0), 0)

    return pl.pallas_call(
        _attn_kernel,
        grid=(n_grp, bsz, nq),
        in_specs=[pl.BlockSpec(blk, cur(0)),
                  pl.BlockSpec(blk, prev(1)), pl.BlockSpec(blk, cur(1)),
                  pl.BlockSpec(blk, prev(2)), pl.BlockSpec(blk, cur(2)),
                  pl.BlockSpec((1, n_head, REL_ROW), lambda p, b, i: (p, 0, 0))],
        out_specs=pl.BlockSpec(blk, lambda p, b, i: (p, b * nq + i, 0)),
        out_shape=jax.ShapeDtypeStruct((n_grp * n_slab, t, LANES), BF16),
        scratch_shapes=[pltpu.VMEM((1 + ATT_QB // ATT_SUB, n_head, ATT_SUB, ATT_KW), F32)],
        compiler_params=_params(("arbitrary", "arbitrary", "arbitrary")),
        name="chunk_attn",
    )(slabs, slabs, slabs, slabs, slabs, rel_rows)


def _rel_rows(rel_bias):
    n_head = rel_bias.shape[0]
    assert rel_bias.shape[1] == 2 * MAX_REL + 1 and n_head % ATT_HEADS == 0
    rb = rel_bias.astype(F32)
    far = rb[:, 2 * MAX_REL:]
    n_far = ATT_KW - 2 * MAX_REL
    row = jnp.concatenate([jnp.broadcast_to(far, (n_head, n_far)), rb[:, :0:-1],
                           jnp.broadcast_to(far, (n_head, REL_ROW - ATT_KW))], axis=1)
    return row.reshape(n_head // ATT_HEADS, ATT_HEADS, REL_ROW)


def _out_kernel(x_ref, a_ref, u_ref, v_ref, ws_ref, bs_ref, ga_ref, gb_ref, gt_ref, post_ref, w_ref, o_ref):
    n_slab = a_ref.shape[0]
    width = n_slab * LANES
    tm = x_ref.shape[0]
    ti = lax.broadcasted_iota(jnp.int32, (GATE_BLOCK, GATE_BLOCK), 0)
    si = lax.broadcasted_iota(jnp.int32, (GATE_BLOCK, GATE_BLOCK), 1)
    causal = (ti // CHUNK) >= (si // CHUNK)
    w_gate = [jnp.where(causal, ws_ref[g], 0.0).astype(BF16) for g in range(n_slab)]

    def gated(r):
        blocks = [slice(s, s + GATE_BLOCK) for s in range(r.start, r.stop, GATE_BLOCK)]
        groups = []
        for g in range(n_slab):
            v_wide = jnp.concatenate([v_ref[g, blk, :] for blk in blocks], axis=1)
            z = jnp.dot(w_gate[g], v_wide, preferred_element_type=F32)
            groups.append(jnp.concatenate(
                [u_ref[g, blk, :].astype(F32) * (z[:, n * LANES:(n + 1) * LANES] + bs_ref[g])
                 for n, blk in enumerate(blocks)], axis=0))
        return groups

    def normed(values, g_ref):
        ss = sum(jnp.sum(v * v, axis=-1, keepdims=True) for v in values)
        inv = lax.rsqrt(ss * (1.0 / width) + EPS)
        return [(v * inv * g_ref[:, p * LANES:(p + 1) * LANES]).astype(BF16) for p, v in enumerate(values)]

    coef = gt_ref[0] * post_ref[...]
    for r in (slice(0, tm // 2), slice(tm // 2, tm)):
        attn = [a_ref[p, r, :].astype(F32) for p in range(n_slab)]
        merged = jnp.concatenate(normed(attn, ga_ref) + normed(gated(r), gb_ref), axis=1)
        y = jnp.dot(merged, w_ref[...], preferred_element_type=F32)
        o_ref[r, :] = x_ref[r, :] + coef * _rms(y)


def _out_proj(x2, seq, a_slabs, slabs, w_s, b_s, g_a, g_b, gt, post_g, w_out):
    t, d = x2.shape
    n_slab = a_slabs.shape[0]
    assert w_s.shape == (n_slab, GATE_BLOCK, GATE_BLOCK)
    tm = min(OUT_TM, seq)
    per_b = seq // tm
    b_bc = jnp.broadcast_to(b_s.astype(F32)[:, :, None], (n_slab, GATE_BLOCK, LANES))
    half_spec = pl.BlockSpec((1, n_slab * LANES), lambda i: (0, 0))
    const3 = lambda i: (0, 0, 0)
    return pl.pallas_call(
        _out_kernel,
        grid=(t // tm,),
        in_specs=[pl.BlockSpec((tm, d), lambda i: (i, 0)),
                  pl.BlockSpec((n_slab, tm, LANES), lambda i: (0, i, 0)),
                  pl.BlockSpec((n_slab, tm, LANES), lambda i: (3, i, 0)),
                  pl.BlockSpec((n_slab, tm, LANES), lambda i: (4, i, 0)),
                  pl.BlockSpec((n_slab, GATE_BLOCK, GATE_BLOCK), const3),
                  pl.BlockSpec((n_slab, GATE_BLOCK, LANES), const3),
                  half_spec, half_spec,
                  pl.BlockSpec((1, 1, d), lambda i: (i // per_b, 0, 0)),
                  pl.BlockSpec((1, d), lambda i: (0, 0)),
                  pl.BlockSpec((d, d), lambda i: (0, 0), pipeline_mode=pl.Buffered(1))],
        out_specs=pl.BlockSpec((tm, d), lambda i: (i, 0)),
        out_shape=jax.ShapeDtypeStruct((t, d), F32),
        compiler_params=_params(("parallel",)),
        name="gate_out_proj",
    )(x2, a_slabs, slabs, slabs, w_s, b_bc, g_a, g_b, gt, post_g, w_out)


def kernel(x, c, w_ada, b_ada, ffn1_pre_g, ffn1_post_g, ffn1_w_gu, ffn1_w_down, mix_pre_g, mix_post_g, w_in, rel_bias, ln_v_g, ln_v_b, w_s, b_s, g_out_a, g_out_b, w_out, ffn2_pre_g, ffn2_post_g, ffn2_w_gu, ffn2_w_down):
    bsz, seq, d = x.shape
    depth = w_ada.shape[0]
    assert seq % ATT_QB == 0 and seq % min(PROJ_TM, seq) == 0
    x2 = x.reshape(bsz * seq, d)
    row = lambda v: v.reshape(1, -1)
    for l in range(depth):
        mod = _ada(c, w_ada[l], b_ada[l]).reshape(bsz, N_MOD, 1, d)
        sh1, sc1, gt1, sh2, sc2, gt2, sh3, sc3, gt3 = [mod[:, i] for i in range(N_MOD)]

        jobs = (("gu", ffn2_w_gu[l]), ("cast", ffn2_w_down[l]), ("cast", w_in[l]), ("cast", w_out[l]))
        x2, (wgu2, wd2, w_in_bf, w_out_bf) = _ffn(
            x2, seq, sh1, sc1, gt1, row(ffn1_pre_g[l]), row(ffn1_post_g[l]),
            _pack_gu(ffn1_w_gu[l], FFN_TF), ffn1_w_down[l].astype(BF16), jobs)

        slabs = _proj(x2, seq, sh2, sc2, row(mix_pre_g[l]), row(ln_v_g[l]), row(ln_v_b[l]), w_in_bf)
        out_a = _attn(slabs, _rel_rows(rel_bias[l]), bsz, seq)
        x2 = _out_proj(x2, seq, out_a, slabs, w_s[l], b_s[l], row(g_out_a[l]), row(g_out_b[l]), gt2,
                       row(mix_post_g[l]), w_out_bf)

        x2, _ = _ffn(x2, seq, sh3, sc3, gt3, row(ffn2_pre_g[l]), row(ffn2_post_g[l]), wgu2, wd2)
    return x2.reshape(bsz, seq, d)
```

```python
import functools
import math

import jax
import jax.numpy as jnp
from jax import lax
from jax.experimental import pallas as pl
from jax.experimental.pallas import tpu as pltpu

F32 = jnp.float32
BF16 = jnp.bfloat16

EPS = 1e-6
CHUNK = 64
LEFT_CHUNKS = 8
HEAD_DIM = 64
MAX_REL = 128
GATE_BLOCK = 128
N_MOD = 9

LANES = 128
BF16_SUBLANES = 16
F32_SUBLANES = 8
PACK_ROWS = 128
ADA_TN = 1024
VMEM_LIMIT = 58 * 1024 * 1024

FFN_TM = 1024
FFN_TF = 512
PROJ_TM = 512
PROJ_SECTIONS = 5
ATT_QB = 512
ATT_SUB = 128
ATT_HEADS = 16
ATT_KW = ATT_SUB + LEFT_CHUNKS * CHUNK
REL_ROW = ATT_KW + ATT_SUB
OUT_TM = 512
NEG = -1e30
LOG2E = math.log2(math.e)


def _params(sem, vmem=VMEM_LIMIT):
    return pltpu.CompilerParams(dimension_semantics=sem, vmem_limit_bytes=vmem)


def _rms(v):
    return v * lax.rsqrt(jnp.mean(v * v, axis=-1, keepdims=True) + EPS)


def _modulated(x, gain, scale, shift):
    return _rms(x) * (gain * (1.0 + scale)) + shift


def _gelu(v):
    return 0.5 * v * (1.0 + lax.erf(v * (1.0 / math.sqrt(2.0))))


def _ada_kernel(c_ref, w_ref, b_ref, o_ref):
    c = c_ref[...]
    ca = (c * jax.nn.sigmoid(c)).astype(BF16)
    o_ref[...] = jnp.dot(ca, w_ref[...].astype(BF16), preferred_element_type=F32) + b_ref[...]


def _ada(c, w, b):
    bsz, d = c.shape
    n = w.shape[1]
    rows = F32_SUBLANES
    assert bsz <= rows and n % ADA_TN == 0
    cp = jnp.zeros((rows, d), F32).at[:bsz].set(c)
    tn = ADA_TN
    out = pl.pallas_call(
        _ada_kernel,
        grid=(n // tn,),
        in_specs=[pl.BlockSpec((rows, d), lambda j: (0, 0)),
                  pl.BlockSpec((d, tn), lambda j: (0, j)),
                  pl.BlockSpec((1, tn), lambda j: (0, j))],
        out_specs=pl.BlockSpec((rows, tn), lambda j: (0, j)),
        out_shape=jax.ShapeDtypeStruct((rows, n), F32),
        compiler_params=_params(("arbitrary",)),
        name="adaln",
    )(cp, w, b.reshape(1, n))
    return out[:bsz]


def _pack_gu_rows(src_ref, dst_ref):
    n_f, _, two_tf = dst_ref.shape
    tf = two_tf // 2
    d_ff = src_ref.shape[1] // 2
    for f in range(n_f):
        w = min(tf, d_ff - f * tf)
        dst_ref[f, :, :w] = src_ref[:, f * tf:f * tf + w].astype(BF16)
        dst_ref[f, :, w:2 * w] = src_ref[:, d_ff + f * tf:d_ff + f * tf + w].astype(BF16)
        if w < tf:
            dst_ref[f, :, 2 * w:] = jnp.zeros((dst_ref.shape[1], two_tf - 2 * w), BF16)


def _pack_gu(w_gu, tf):
    d, two_f = w_gu.shape
    n_f = pl.cdiv(two_f // 2, tf)
    rb = PACK_ROWS
    return pl.pallas_call(
        _pack_gu_rows,
        grid=(d // rb,),
        in_specs=[pl.BlockSpec((rb, two_f), lambda r: (r, 0))],
        out_specs=pl.BlockSpec((n_f, rb, 2 * tf), lambda r: (0, r, 0)),
        out_shape=jax.ShapeDtypeStruct((n_f, d, 2 * tf), BF16),
        compiler_params=_params(("parallel",)),
        name="pack_gu",
    )(w_gu)


def _ffn_kernel(*refs, last_width, job_kinds):
    n_job = len(job_kinds)
    x_ref, sh_ref, sc_ref, gt_ref, pre_ref, post_ref, wgu_ref, wd_ref = refs[:8]
    job_src = refs[8:8 + n_job]
    o_ref = refs[8 + n_job]
    job_dst = refs[9 + n_job:9 + 2 * n_job]
    (h_ref,) = refs[9 + 2 * n_job:]
    acc_ref = o_ref
    step = pl.program_id(1)
    n_f = pl.num_programs(1)
    tf = wd_ref.shape[0]
    is_partial = _ffn_tile(pl.program_id(0), step, n_f) == n_f - 1

    def side_jobs():
        for kind, src, dst in zip(job_kinds, job_src, job_dst):
            if kind == "gu":
                _pack_gu_rows(src, dst)
            else:
                dst[...] = src[...].astype(BF16)

    def down_input(h, width):
        gu = jnp.dot(h, wgu_ref[0, :, :2 * width], preferred_element_type=F32)
        g, u = gu[:, :width], gu[:, width:]
        return (g * jax.nn.sigmoid(g) * u).astype(BF16)

    tm = x_ref.shape[0]
    halves = (slice(0, tm // 2), slice(tm // 2, tm))

    def accumulate(width):
        acts = [down_input(h_ref[r, :], width) for r in halves]
        for r, a in zip(halves, acts):
            acc_ref[r, :] += jnp.dot(a, wd_ref[:width, :], preferred_element_type=F32)

    @pl.when(step == 0)
    def _():
        for r in halves:
            h = _modulated(x_ref[r, :], pre_ref[...], sc_ref[0], sh_ref[0]).astype(BF16)
            h_ref[r, :] = h
            acc_ref[r, :] = jnp.dot(down_input(h, tf), wd_ref[...], preferred_element_type=F32)
        side_jobs()

    @pl.when((step > 0) & (step < n_f - 1) & jnp.logical_not(is_partial))
    def _():
        accumulate(tf)
        side_jobs()

    @pl.when(is_partial)
    def _():
        accumulate(last_width)
        side_jobs()

    @pl.when(step == n_f - 1)
    def _():
        for r in halves:
            a = down_input(h_ref[r, :], tf)
            y = acc_ref[r, :] + jnp.dot(a, wd_ref[...], preferred_element_type=F32)
            o_ref[r, :] = x_ref[r, :] + (0.5 * gt_ref[0] * post_ref[...]) * _rms(y)
        side_jobs()


def _ffn_tile(i, step, n_f):
    mid = (n_f - 1) // 2
    pos = jnp.where(i % 2 == 0, step, n_f - 1 - step)
    return jnp.where(pos == mid, n_f - 1, jnp.where(pos < mid, pos, pos - 1))


def _job_rows(n_rows, n_steps):
    rb = BF16_SUBLANES
    while n_rows % rb or n_rows // rb > n_steps:
        rb += BF16_SUBLANES
    return rb


def _ffn(x2, seq, sh, sc, gt, pre_g, post_g, wgu, wd, jobs=()):
    t, d = x2.shape
    d_ff = wd.shape[0]
    tm, tf = min(FFN_TM, seq), wgu.shape[2] // 2
    n_f = wgu.shape[0]
    assert n_f >= 3 and n_f == pl.cdiv(d_ff, tf)
    n_steps = (t // tm) * n_f
    per_b = seq // tm
    mod_spec = pl.BlockSpec((1, 1, d), lambda i, f: (i // per_b, 0, 0))
    vec_spec = pl.BlockSpec((1, d), lambda i, f: (0, 0))

    job_in, job_out, job_shape = [], [], []
    for kind, w in jobs:
        rows, cols = w.shape
        rb = _job_rows(rows, n_steps)
        blk = lambda i, f, n=rows // rb: jnp.minimum(i * n_f + f, n - 1)
        job_in.append(pl.BlockSpec((rb, cols), lambda i, f, blk=blk: (blk(i, f), 0)))
        if kind == "gu":
            job_out.append(pl.BlockSpec((n_f, rb, 2 * tf), lambda i, f, blk=blk: (0, blk(i, f), 0)))
            job_shape.append(jax.ShapeDtypeStruct((n_f, rows, 2 * tf), BF16))
        else:
            job_out.append(pl.BlockSpec((rb, cols), lambda i, f, blk=blk: (blk(i, f), 0)))
            job_shape.append(jax.ShapeDtypeStruct((rows, cols), BF16))

    out = pl.pallas_call(
        functools.partial(_ffn_kernel, last_width=d_ff - (n_f - 1) * tf,
                          job_kinds=tuple(kind for kind, _ in jobs)),
        grid=(t // tm, n_f),
        in_specs=[pl.BlockSpec((tm, d), lambda i, f: (i, 0)),
                  mod_spec, mod_spec, mod_spec, vec_spec, vec_spec,
                  pl.BlockSpec((1, d, 2 * tf), lambda i, f: (_ffn_tile(i, f, n_f), 0, 0)),
                  pl.BlockSpec((tf, d), lambda i, f: (_ffn_tile(i, f, n_f), 0))] + job_in,
        out_specs=[pl.BlockSpec((tm, d), lambda i, f: (i, 0))] + job_out,
        out_shape=[jax.ShapeDtypeStruct((t, d), F32)] + job_shape,
        scratch_shapes=[pltpu.VMEM((tm, d), BF16)],
        compiler_params=_params(("arbitrary", "arbitrary")),
        name="ffn_convert" if jobs else "ffn",
    )(x2, sh, sc, gt, pre_g, post_g, wgu, wd, *[w for _, w in jobs])
    return out[0], out[1:]


def _proj_kernel(x_ref, sh_ref, sc_ref, pre_ref, lng_ref, lnb_ref, w_ref, o_ref):
    n_slab = o_ref.shape[0] // PROJ_SECTIONS
    tn = n_slab * LANES
    h = _modulated(x_ref[...], pre_ref[...], sc_ref[0], sh_ref[0]).astype(BF16)

    def emit(j, epilogue):
        r = jnp.dot(h, w_ref[:, j * tn:(j + 1) * tn], preferred_element_type=F32)
        val = epilogue(r).astype(BF16)
        for p in range(n_slab):
            o_ref[j * n_slab + p] = val[:, p * LANES:(p + 1) * LANES]

    def gelu_layer_norm(v):
        ge = _gelu(v)
        xc = ge - jnp.mean(ge, axis=-1, keepdims=True)
        y = xc * lax.rsqrt(jnp.mean(xc * xc, axis=-1, keepdims=True) + EPS)
        return y * lng_ref[...] + lnb_ref[...]

    emit(4, gelu_layer_norm)
    emit(1, lambda v: v)
    emit(3, _gelu)
    emit(0, lambda v: v * (LOG2E / math.sqrt(HEAD_DIM)))
    emit(2, lambda v: v)


def _proj(x2, seq, sh, sc, pre_g, ln_g, ln_b, w_in):
    t, d = x2.shape
    n = w_in.shape[1]
    tn = n // PROJ_SECTIONS
    n_slab = tn // LANES
    tm = min(PROJ_TM, seq)
    per_b = seq // tm
    mod_spec = pl.BlockSpec((1, 1, d), lambda i: (i // per_b, 0, 0))
    vec_spec = pl.BlockSpec((1, d), lambda i: (0, 0))
    ln_spec = pl.BlockSpec((1, tn), lambda i: (0, 0))
    return pl.pallas_call(
        _proj_kernel,
        grid=(t // tm,),
        in_specs=[pl.BlockSpec((tm, d), lambda i: (i, 0)),
                  mod_spec, mod_spec, vec_spec, ln_spec, ln_spec,
                  pl.BlockSpec((d, n), lambda i: (0, 0), pipeline_mode=pl.Buffered(1))],
        out_specs=pl.BlockSpec((PROJ_SECTIONS * n_slab, tm, LANES), lambda i: (0, i, 0)),
        out_shape=jax.ShapeDtypeStruct((PROJ_SECTIONS * n_slab, t, LANES), BF16),
        compiler_params=_params(("parallel",)),
        name="in_proj",
    )(x2, sh, sc, pre_g, ln_g, ln_b, w_in)


def _attn_kernel(q_ref, kp_ref, kc_ref, vp_ref, vc_ref, rel_ref, o_ref, tab_ref):
    n_sub = ATT_QB // ATT_SUB
    n_slab = q_ref.shape[0]
    n_head = tab_ref.shape[1]

    @pl.when((pl.program_id(1) == 0) & (pl.program_id(2) == 0))
    def _():
        r = lax.broadcasted_iota(jnp.int32, (ATT_SUB, ATT_KW), 0)
        c = lax.broadcasted_iota(jnp.int32, (ATT_SUB, ATT_KW), 1)
        lo = (r // CHUNK) * CHUNK
        in_band = (c >= lo) & (c < lo + (LEFT_CHUNKS + 1) * CHUNK)
        for h in range(n_head):
            row = (rel_ref[0, h:h + 1, :] - rel_ref[0, h:h + 1, 0:1]) * LOG2E
            base = jnp.broadcast_to(row, (ATT_SUB, REL_ROW))
            toe = pltpu.roll(base, 0, 1, stride=1, stride_axis=0)[:, :ATT_KW]
            t0 = jnp.where(in_band, toe, NEG)
            tab_ref[0, h] = t0
            for i in range(n_sub):
                tab_ref[1 + i, h] = jnp.where(c + i * ATT_SUB >= ATT_QB, t0, NEG)

    qblk = pl.program_id(2)

    pair = LANES // HEAD_DIM
    upper = lax.broadcasted_iota(jnp.int32, (ATT_SUB, LANES), 1) >= HEAD_DIM
    ahead = 2
    near = ATT_KW - 2 * MAX_REL

    def attend_slab(slab, sequence_start):
        q = q_ref[slab]
        k = jnp.concatenate([kp_ref[slab], kc_ref[slab]], axis=0)
        v = jnp.concatenate([vp_ref[slab], vc_ref[slab]], axis=0)
        heads = slice(slab * pair, (slab + 1) * pair)

        def scores(i):
            qs = q[i * ATT_SUB:(i + 1) * ATT_SUB].astype(F32)
            qh = jnp.concatenate([jnp.where(upper, 0.0, qs), jnp.where(upper, qs, 0.0)], axis=0)
            ks = k[i * ATT_SUB:i * ATT_SUB + ATT_KW]
            return lax.dot_general(qh.astype(BF16), ks, (((1,), (1,)), ((), ())), preferred_element_type=F32)

        pending = [scores(i) for i in range(ahead)]
        for i in range(n_sub):
            r0 = i * ATT_SUB
            vs = v[r0:r0 + ATT_KW]
            s = pending.pop(0).reshape(pair, ATT_SUB, ATT_KW)
            if i + ahead < n_sub:
                pending.append(scores(i + ahead))
            if sequence_start:
                s = s + tab_ref[1 + i, heads]
            else:
                s = jnp.concatenate([s[..., :LANES] + tab_ref[0, heads, :, :LANES], s[..., LANES:near],
                                     s[..., near:] + tab_ref[0, heads, :, near:]], axis=-1)
            m = jnp.max(s, axis=-1, keepdims=True)
            p = jnp.exp2(s - m)
            l = jnp.sum(p, axis=-1, keepdims=True)
            pv = jnp.dot(p.reshape(pair * ATT_SUB, ATT_KW).astype(BF16), vs, preferred_element_type=F32)
            pv = pv.reshape(pair, ATT_SUB, LANES) / l
            o_ref[slab, r0:r0 + ATT_SUB, :] = jnp.where(upper, pv[1], pv[0]).astype(o_ref.dtype)

    def attend(sequence_start):
        for slab in range(n_slab):
            attend_slab(slab, sequence_start)

    @pl.when(qblk == 0)
    def _():
        attend(True)

    @pl.when(qblk > 0)
    def _():
        attend(False)


def _attn(slabs, rel_rows, bsz, seq):
    n_grp, n_head = rel_rows.shape[:2]
    n_slab = n_head * HEAD_DIM // LANES
    t = slabs.shape[1]
    nq = seq // ATT_QB
    blk = (n_slab, ATT_QB, LANES)

    def cur(section):
        return lambda p, b, i: (section * n_grp + p, b * nq + i, 0)

    def prev(section):
        return lambda p, b, i: (section * n_grp + p, b * nq + jnp.maximum(i - 1, 0), 0)

    return pl.pallas_call(
        _attn_kernel,
        grid=(n_grp, bsz, nq),
        in_specs=[pl.BlockSpec(blk, cur(0)),
                  pl.BlockSpec(blk, prev(1)), pl.BlockSpec(blk, cur(1)),
                  pl.BlockSpec(blk, prev(2)), pl.BlockSpec(blk, cur(2)),
                  pl.BlockSpec((1, n_head, REL_ROW), lambda p, b, i: (p, 0, 0))],
        out_specs=pl.BlockSpec(blk, lambda p, b, i: (p, b * nq + i, 0)),
        out_shape=jax.ShapeDtypeStruct((n_grp * n_slab, t, LANES), BF16),
        scratch_shapes=[pltpu.VMEM((1 + ATT_QB // ATT_SUB, n_head, ATT_SUB, ATT_KW), F32)],
        compiler_params=_params(("arbitrary", "arbitrary", "arbitrary")),
        name="chunk_attn",
    )(slabs, slabs, slabs, slabs, slabs, rel_rows)


def _rel_rows(rel_bias):
    n_head = rel_bias.shape[0]
    assert rel_bias.shape[1] == 2 * MAX_REL + 1 and n_head % ATT_HEADS == 0
    rb = rel_bias.astype(F32)
    far = rb[:, 2 * MAX_REL:]
    n_far = ATT_KW - 2 * MAX_REL
    row = jnp.concatenate([jnp.broadcast_to(far, (n_head, n_far)), rb[:, :0:-1],
                           jnp.broadcast_to(far, (n_head, REL_ROW - ATT_KW))], axis=1)
    return row.reshape(n_head // ATT_HEADS, ATT_HEADS, REL_ROW)


def _out_kernel(x_ref, a_ref, u_ref, v_ref, ws_ref, bs_ref, ga_ref, gb_ref, gt_ref, post_ref, w_ref, o_ref):
    n_slab = a_ref.shape[0]
    width = n_slab * LANES
    tm = x_ref.shape[0]
    ti = lax.broadcasted_iota(jnp.int32, (GATE_BLOCK, GATE_BLOCK), 0)
    si = lax.broadcasted_iota(jnp.int32, (GATE_BLOCK, GATE_BLOCK), 1)
    causal = (ti // CHUNK) >= (si // CHUNK)
    w_gate = [jnp.where(causal, ws_ref[g], 0.0).astype(BF16) for g in range(n_slab)]

    def gated(r):
        blocks = [slice(s, s + GATE_BLOCK) for s in range(r.start, r.stop, GATE_BLOCK)]
        groups = []
        for g in range(n_slab):
            v_wide = jnp.concatenate([v_ref[g, blk, :] for blk in blocks], axis=1)
            z = jnp.dot(w_gate[g], v_wide, preferred_element_type=F32)
            groups.append(jnp.concatenate(
                [u_ref[g, blk, :].astype(F32) * (z[:, n * LANES:(n + 1) * LANES] + bs_ref[g])
                 for n, blk in enumerate(blocks)], axis=0))
        return groups

    def normed(values, g_ref):
        ss = sum(jnp.sum(v * v, axis=-1, keepdims=True) for v in values)
        inv = lax.rsqrt(ss * (1.0 / width) + EPS)
        return [(v * inv * g_ref[:, p * LANES:(p + 1) * LANES]).astype(BF16) for p, v in enumerate(values)]

    coef = gt_ref[0] * post_ref[...]
    for r in (slice(0, tm // 2), slice(tm // 2, tm)):
        attn = [a_ref[p, r, :].astype(F32) for p in range(n_slab)]
        merged = jnp.concatenate(normed(attn, ga_ref) + normed(gated(r), gb_ref), axis=1)
        y = jnp.dot(merged, w_ref[...], preferred_element_type=F32)
        o_ref[r, :] = x_ref[r, :] + coef * _rms(y)


def _out_proj(x2, seq, a_slabs, slabs, w_s, b_s, g_a, g_b, gt, post_g, w_out):
    t, d = x2.shape
    n_slab = a_slabs.shape[0]
    assert w_s.shape == (n_slab, GATE_BLOCK, GATE_BLOCK)
    tm = min(OUT_TM, seq)
    per_b = seq // tm
    b_bc = jnp.broadcast_to(b_s.astype(F32)[:, :, None], (n_slab, GATE_BLOCK, LANES))
    half_spec = pl.BlockSpec((1, n_slab * LANES), lambda i: (0, 0))
    const3 = lambda i: (0, 0, 0)
    return pl.pallas_call(
        _out_kernel,
        grid=(t // tm,),
        in_specs=[pl.BlockSpec((tm, d), lambda i: (i, 0)),
                  pl.BlockSpec((n_slab, tm, LANES), lambda i: (0, i, 0)),
                  pl.BlockSpec((n_slab, tm, LANES), lambda i: (3, i, 0)),
                  pl.BlockSpec((n_slab, tm, LANES), lambda i: (4, i, 0)),
                  pl.BlockSpec((n_slab, GATE_BLOCK, GATE_BLOCK), const3),
                  pl.BlockSpec((n_slab, GATE_BLOCK, LANES), const3),
                  half_spec, half_spec,
                  pl.BlockSpec((1, 1, d), lambda i: (i // per_b, 0, 0)),
                  pl.BlockSpec((1, d), lambda i: (0, 0)),
                  pl.BlockSpec((d, d), lambda i: (0, 0), pipeline_mode=pl.Buffered(1))],
        out_specs=pl.BlockSpec((tm, d), lambda i: (i, 0)),
        out_shape=jax.ShapeDtypeStruct((t, d), F32),
        compiler_params=_params(("parallel",)),
        name="gate_out_proj",
    )(x2, a_slabs, slabs, slabs, w_s, b_bc, g_a, g_b, gt, post_g, w_out)


def kernel(x, c, w_ada, b_ada, ffn1_pre_g, ffn1_post_g, ffn1_w_gu, ffn1_w_down, mix_pre_g, mix_post_g, w_in, rel_bias, ln_v_g, ln_v_b, w_s, b_s, g_out_a, g_out_b, w_out, ffn2_pre_g, ffn2_post_g, ffn2_w_gu, ffn2_w_down):
    bsz, seq, d = x.shape
    depth = w_ada.shape[0]
    assert seq % ATT_QB == 0 and seq % min(PROJ_TM, seq) == 0
    x2 = x.reshape(bsz * seq, d)
    row = lambda v: v.reshape(1, -1)
    for l in range(depth):
        mod = _ada(c, w_ada[l], b_ada[l]).reshape(bsz, N_MOD, 1, d)
        sh1, sc1, gt1, sh2, sc2, gt2, sh3, sc3, gt3 = [mod[:, i] for i in range(N_MOD)]

        jobs = (("gu", ffn2_w_gu[l]), ("cast", ffn2_w_down[l]), ("cast", w_in[l]), ("cast", w_out[l]))
        x2, (wgu2, wd2, w_in_bf, w_out_bf) = _ffn(
            x2, seq, sh1, sc1, gt1, row(ffn1_pre_g[l]), row(ffn1_post_g[l]),
            _pack_gu(ffn1_w_gu[l], FFN_TF), ffn1_w_down[l].astype(BF16), jobs)

        slabs = _proj(x2, seq, sh2, sc2, row(mix_pre_g[l]), row(ln_v_g[l]), row(ln_v_b[l]), w_in_bf)
        out_a = _attn(slabs, _rel_rows(rel_bias[l]), bsz, seq)
        x2 = _out_proj(x2, seq, out_a, slabs, w_s[l], b_s[l], row(g_out_a[l]), row(g_out_b[l]), gt2,
                       row(mix_post_g[l]), w_out_bf)

        x2, _ = _ffn(x2, seq, sh3, sc3, gt3, row(ffn2_pre_g[l]), row(ffn2_post_g[l]), wgu2, wd2)
    return x2.reshape(bsz, seq, d)
```

```python
import functools
import math

import jax
import jax.numpy as jnp
from jax import lax
from jax.experimental import pallas as pl
from jax.experimental.pallas import tpu as pltpu

F32 = jnp.float32
BF16 = jnp.bfloat16

EPS = 1e-6
CHUNK = 64
LEFT_CHUNKS = 8
HEAD_DIM = 64
MAX_REL = 128
GATE_BLOCK = 128
N_MOD = 9

LANES = 128
BF16_SUBLANES = 16
F32_SUBLANES = 8
PACK_ROWS = 128
ADA_TN = 1024
VMEM_LIMIT = 58 * 1024 * 1024

FFN_TM = 1024
FFN_TF = 512
PROJ_TM = 512
PROJ_SECTIONS = 5
ATT_QB = 512
ATT_SUB = 128
ATT_HEADS = 16
ATT_KW = ATT_SUB + LEFT_CHUNKS * CHUNK
REL_ROW = ATT_KW + ATT_SUB
OUT_TM = 512
NEG = -1e30
LOG2E = math.log2(math.e)


def _params(sem, vmem=VMEM_LIMIT):
    return pltpu.CompilerParams(dimension_semantics=sem, vmem_limit_bytes=vmem)


def _rms(v):
    return v * lax.rsqrt(jnp.mean(v * v, axis=-1, keepdims=True) + EPS)


def _modulated(x, gain, scale, shift):
    return _rms(x) * (gain * (1.0 + scale)) + shift


def _gelu(v):
    return 0.5 * v * (1.0 + lax.erf(v * (1.0 / math.sqrt(2.0))))


def _ada_kernel(c_ref, w_ref, b_ref, o_ref):
    c = c_ref[...]
    ca = (c * jax.nn.sigmoid(c)).astype(BF16)
    o_ref[...] = jnp.dot(ca, w_ref[...].astype(BF16), preferred_element_type=F32) + b_ref[...]


def _ada(c, w, b):
    bsz, d = c.shape
    n = w.shape[1]
    rows = F32_SUBLANES
    assert bsz <= rows and n % ADA_TN == 0
    cp = jnp.zeros((rows, d), F32).at[:bsz].set(c)
    tn = ADA_TN
    out = pl.pallas_call(
        _ada_kernel,
        grid=(n // tn,),
        in_specs=[pl.BlockSpec((rows, d), lambda j: (0, 0)),
                  pl.BlockSpec((d, tn), lambda j: (0, j)),
                  pl.BlockSpec((1, tn), lambda j: (0, j))],
        out_specs=pl.BlockSpec((rows, tn), lambda j: (0, j)),
        out_shape=jax.ShapeDtypeStruct((rows, n), F32),
        compiler_params=_params(("arbitrary",)),
        name="adaln",
    )(cp, w, b.reshape(1, n))
    return out[:bsz]


def _pack_gu_rows(src_ref, dst_ref):
    n_f, _, two_tf = dst_ref.shape
    tf = two_tf // 2
    d_ff = src_ref.shape[1] // 2
    for f in range(n_f):
        w = min(tf, d_ff - f * tf)
        dst_ref[f, :, :w] = src_ref[:, f * tf:f * tf + w].astype(BF16)
        dst_ref[f, :, w:2 * w] = src_ref[:, d_ff + f * tf:d_ff + f * tf + w].astype(BF16)
        if w < tf:
            dst_ref[f, :, 2 * w:] = jnp.zeros((dst_ref.shape[1], two_tf - 2 * w), BF16)


def _pack_gu(w_gu, tf):
    d, two_f = w_gu.shape
    n_f = pl.cdiv(two_f // 2, tf)
    rb = PACK_ROWS
    return pl.pallas_call(
        _pack_gu_rows,
        grid=(d // rb,),
        in_specs=[pl.BlockSpec((rb, two_f), lambda r: (r, 0))],
        out_specs=pl.BlockSpec((n_f, rb, 2 * tf), lambda r: (0, r, 0)),
        out_shape=jax.ShapeDtypeStruct((n_f, d, 2 * tf), BF16),
        compiler_params=_params(("parallel",)),
        name="pack_gu",
    )(w_gu)


def _ffn_kernel(*refs, last_width, job_kinds):
    n_job = len(job_kinds)
    x_ref, sh_ref, sc_ref, gt_ref, pre_ref, post_ref, wgu_ref, wd_ref = refs[:8]
    job_src = refs[8:8 + n_job]
    o_ref = refs[8 + n_job]
    job_dst = refs[9 + n_job:9 + 2 * n_job]
    (h_ref,) = refs[9 + 2 * n_job:]
    acc_ref = o_ref
    step = pl.program_id(1)
    n_f = pl.num_programs(1)
    tf = wd_ref.shape[0]
    is_partial = _ffn_tile(pl.program_id(0), step, n_f) == n_f - 1

    def side_jobs():
        for kind, src, dst in zip(job_kinds, job_src, job_dst):
            if kind == "gu":
                _pack_gu_rows(src, dst)
            else:
                dst[...] = src[...].astype(BF16)

    def down_input(h, width):
        gu = jnp.dot(h, wgu_ref[0, :, :2 * width], preferred_element_type=F32)
        g, u = gu[:, :width], gu[:, width:]
        return (g * jax.nn.sigmoid(g) * u).astype(BF16)

    tm = x_ref.shape[0]
    halves = (slice(0, tm // 2), slice(tm // 2, tm))

    def accumulate(width):
        acts = [down_input(h_ref[r, :], width) for r in halves]
        for r, a in zip(halves, acts):
            acc_ref[r, :] += jnp.dot(a, wd_ref[:width, :], preferred_element_type=F32)

    @pl.when(step == 0)
    def _():
        for r in halves:
            h = _modulated(x_ref[r, :], pre_ref[...], sc_ref[0], sh_ref[0]).astype(BF16)
            h_ref[r, :] = h
            acc_ref[r, :] = jnp.dot(down_input(h, tf), wd_ref[...], preferred_element_type=F32)
        side_jobs()

    @pl.when((step > 0) & (step < n_f - 1) & jnp.logical_not(is_partial))
    def _():
        accumulate(tf)
        side_jobs()

    @pl.when(is_partial)
    def _():
        accumulate(last_width)
        side_jobs()

    @pl.when(step == n_f - 1)
    def _():
        for r in halves:
            a = down_input(h_ref[r, :], tf)
            y = acc_ref[r, :] + jnp.dot(a, wd_ref[...], preferred_element_type=F32)
            o_ref[r, :] = x_ref[r, :] + (0.5 * gt_ref[0] * post_ref[...]) * _rms(y)
        side_jobs()


def _ffn_tile(i, step, n_f):
    mid = (n_f - 1) // 2
    pos = jnp.where(i % 2 == 0, step, n_f - 1 - step)
    return jnp.where(pos == mid, n_f - 1, jnp.where(pos < mid, pos, pos - 1))


def _job_rows(n_rows, n_steps):
    rb = BF16_SUBLANES
    while n_rows % rb or n_rows // rb > n_steps:
        rb += BF16_SUBLANES
    return rb


def _ffn(x2, seq, sh, sc, gt, pre_g, post_g, wgu, wd, jobs=()):
    t, d = x2.shape
    d_ff = wd.shape[0]
    tm, tf = min(FFN_TM, seq), wgu.shape[2] // 2
    n_f = wgu.shape[0]
    assert n_f >= 3 and n_f == pl.cdiv(d_ff, tf)
    n_steps = (t // tm) * n_f
    per_b = seq // tm
    mod_spec = pl.BlockSpec((1, 1, d), lambda i, f: (i // per_b, 0, 0))
    vec_spec = pl.BlockSpec((1, d), lambda i, f: (0, 0))

    job_in, job_out, job_shape = [], [], []
    for kind, w in jobs:
        rows, cols = w.shape
        rb = _job_rows(rows, n_steps)
        blk = lambda i, f, n=rows // rb: jnp.minimum(i * n_f + f, n - 1)
        job_in.append(pl.BlockSpec((rb, cols), lambda i, f, blk=blk: (blk(i, f), 0)))
        if kind == "gu":
            job_out.append(pl.BlockSpec((n_f, rb, 2 * tf), lambda i, f, blk=blk: (0, blk(i, f), 0)))
            job_shape.append(jax.ShapeDtypeStruct((n_f, rows, 2 * tf), BF16))
        else:
            job_out.append(pl.BlockSpec((rb, cols), lambda i, f, blk=blk: (blk(i, f), 0)))
            job_shape.append(jax.ShapeDtypeStruct((rows, cols), BF16))

    out = pl.pallas_call(
        functools.partial(_ffn_kernel, last_width=d_ff - (n_f - 1) * tf,
                          job_kinds=tuple(kind for kind, _ in jobs)),
        grid=(t // tm, n_f),
        in_specs=[pl.BlockSpec((tm, d), lambda i, f: (i, 0)),
                  mod_spec, mod_spec, mod_spec, vec_spec, vec_spec,
                  pl.BlockSpec((1, d, 2 * tf), lambda i, f: (_ffn_tile(i, f, n_f), 0, 0)),
                  pl.BlockSpec((tf, d), lambda i, f: (_ffn_tile(i, f, n_f), 0))] + job_in,
        out_specs=[pl.BlockSpec((tm, d), lambda i, f: (i, 0))] + job_out,
        out_shape=[jax.ShapeDtypeStruct((t, d), F32)] + job_shape,
        scratch_shapes=[pltpu.VMEM((tm, d), BF16)],
        compiler_params=_params(("arbitrary", "arbitrary")),
        name="ffn_convert" if jobs else "ffn",
    )(x2, sh, sc, gt, pre_g, post_g, wgu, wd, *[w for _, w in jobs])
    return out[0], out[1:]


def _proj_kernel(x_ref, sh_ref, sc_ref, pre_ref, lng_ref, lnb_ref, w_ref, o_ref):
    n_slab = o_ref.shape[0] // PROJ_SECTIONS
    tn = n_slab * LANES
    h = _modulated(x_ref[...], pre_ref[...], sc_ref[0], sh_ref[0]).astype(BF16)

    def emit(j, epilogue):
        r = jnp.dot(h, w_ref[:, j * tn:(j + 1) * tn], preferred_element_type=F32)
        val = epilogue(r).astype(BF16)
        for p in range(n_slab):
            o_ref[j * n_slab + p] = val[:, p * LANES:(p + 1) * LANES]

    def gelu_layer_norm(v):
        ge = _gelu(v)
        xc = ge - jnp.mean(ge, axis=-1, keepdims=True)
        y = xc * lax.rsqrt(jnp.mean(xc * xc, axis=-1, keepdims=True) + EPS)
        return y * lng_ref[...] + lnb_ref[...]

    emit(4, gelu_layer_norm)
    emit(1, lambda v: v)
    emit(3, _gelu)
    emit(0, lambda v: v * (LOG2E / math.sqrt(HEAD_DIM)))
    emit(2, lambda v: v)


def _proj(x2, seq, sh, sc, pre_g, ln_g, ln_b, w_in):
    t, d = x2.shape
    n = w_in.shape[1]
    tn = n // PROJ_SECTIONS
    n_slab = tn // LANES
    tm = min(PROJ_TM, seq)
    per_b = seq // tm
    mod_spec = pl.BlockSpec((1, 1, d), lambda i: (i // per_b, 0, 0))
    vec_spec = pl.BlockSpec((1, d), lambda i: (0, 0))
    ln_spec = pl.BlockSpec((1, tn), lambda i: (0, 0))
    return pl.pallas_call(
        _proj_kernel,
        grid=(t // tm,),
        in_specs=[pl.BlockSpec((tm, d), lambda i: (i, 0)),
                  mod_spec, mod_spec, vec_spec, ln_spec, ln_spec,
                  pl.BlockSpec((d, n), lambda i: (0, 0), pipeline_mode=pl.Buffered(1))],
        out_specs=pl.BlockSpec((PROJ_SECTIONS * n_slab, tm, LANES), lambda i: (0, i, 0)),
        out_shape=jax.ShapeDtypeStruct((PROJ_SECTIONS * n_slab, t, LANES), BF16),
        compiler_params=_params(("parallel",)),
        name="in_proj",
    )(x2, sh, sc, pre_g, ln_g, ln_b, w_in)


def _attn_kernel(q_ref, kp_ref, kc_ref, vp_ref, vc_ref, rel_ref, o_ref, tab_ref):
    n_sub = ATT_QB // ATT_SUB
    n_slab = q_ref.shape[0]
    n_head = tab_ref.shape[1]

    @pl.when((pl.program_id(1) == 0) & (pl.program_id(2) == 0))
    def _():
        r = lax.broadcasted_iota(jnp.int32, (ATT_SUB, ATT_KW), 0)
        c = lax.broadcasted_iota(jnp.int32, (ATT_SUB, ATT_KW), 1)
        lo = (r // CHUNK) * CHUNK
        in_band = (c >= lo) & (c < lo + (LEFT_CHUNKS + 1) * CHUNK)
        for h in range(n_head):
            row = (rel_ref[0, h:h + 1, :] - rel_ref[0, h:h + 1, 0:1]) * LOG2E
            base = jnp.broadcast_to(row, (ATT_SUB, REL_ROW))
            toe = pltpu.roll(base, 0, 1, stride=1, stride_axis=0)[:, :ATT_KW]
            t0 = jnp.where(in_band, toe, NEG)
            tab_ref[0, h] = t0
            for i in range(n_sub):
                tab_ref[1 + i, h] = jnp.where(c + i * ATT_SUB >= ATT_QB, t0, NEG)

    qblk = pl.program_id(2)

    pair = LANES // HEAD_DIM
    upper = lax.broadcasted_iota(jnp.int32, (ATT_SUB, LANES), 1) >= HEAD_DIM
    ahead = 2
    near = ATT_KW - 2 * MAX_REL

    def attend_slab(slab, sequence_start):
        q = q_ref[slab]
        k = jnp.concatenate([kp_ref[slab], kc_ref[slab]], axis=0)
        v = jnp.concatenate([vp_ref[slab], vc_ref[slab]], axis=0)
        heads = slice(slab * pair, (slab + 1) * pair)

        def scores(i):
            qs = q[i * ATT_SUB:(i + 1) * ATT_SUB].astype(F32)
            qh = jnp.concatenate([jnp.where(upper, 0.0, qs), jnp.where(upper, qs, 0.0)], axis=0)
            ks = k[i * ATT_SUB:i * ATT_SUB + ATT_KW]
            return lax.dot_general(qh.astype(BF16), ks, (((1,), (1,)), ((), ())), preferred_element_type=F32)

        pending = [scores(i) for i in range(ahead)]
        for i in range(n_sub):
            r0 = i * ATT_SUB
            vs = v[r0:r0 + ATT_KW]
            s = pending.pop(0).reshape(pair, ATT_SUB, ATT_KW)
            if i + ahead < n_sub:
                pending.append(scores(i + ahead))
            if sequence_start:
                s = s + tab_ref[1 + i, heads]
            else:
                s = jnp.concatenate([s[..., :LANES] + tab_ref[0, heads, :, :LANES], s[..., LANES:near],
                                     s[..., near:] + tab_ref[0, heads, :, near:]], axis=-1)
            m = jnp.max(s, axis=-1, keepdims=True)
            p = jnp.exp2(s - m)
            l = jnp.sum(p, axis=-1, keepdims=True)
            pv = jnp.dot(p.reshape(pair * ATT_SUB, ATT_KW).astype(BF16), vs, preferred_element_type=F32)
            pv = pv.reshape(pair, ATT_SUB, LANES) / l
            o_ref[slab, r0:r0 + ATT_SUB, :] = jnp.where(upper, pv[1], pv[0]).astype(o_ref.dtype)

    def attend(sequence_start):
        for slab in range(n_slab):
            attend_slab(slab, sequence_start)

    @pl.when(qblk == 0)
    def _():
        attend(True)

    @pl.when(qblk > 0)
    def _():
        attend(False)


def _attn(slabs, rel_rows, bsz, seq):
    n_grp, n_head = rel_rows.shape[:2]
    n_slab = n_head * HEAD_DIM // LANES
    t = slabs.shape[1]
    nq = seq // ATT_QB
    blk = (n_slab, ATT_QB, LANES)

    def cur(section):
        return lambda p, b, i: (section * n_grp + p, b * nq + i, 0)

    def prev(section):
        return lambda p, b, i: (section * n_grp + p, b * nq + jnp.maximum(i - 1, 0), 0)

    return pl.pallas_call(
        _attn_kernel,
        grid=(n_grp, bsz, nq),
        in_specs=[pl.BlockSpec(blk, cur(0)),
                  pl.BlockSpec(blk, prev(1)), pl.BlockSpec(blk, cur(1)),
                  pl.BlockSpec(blk, prev(2)), pl.BlockSpec(blk, cur(2)),
                  pl.BlockSpec((1, n_head, REL_ROW), lambda p, b, i: (p, 0, 0))],
        out_specs=pl.BlockSpec(blk, lambda p, b, i: (p, b * nq + i, 0)),
        out_shape=jax.ShapeDtypeStruct((n_grp * n_slab, t, LANES), BF16),
        scratch_shapes=[pltpu.VMEM((1 + ATT_QB // ATT_SUB, n_head, ATT_SUB, ATT_KW), F32)],
        compiler_params=_params(("arbitrary", "arbitrary", "arbitrary")),
        name="chunk_attn",
    )(slabs, slabs, slabs, slabs, slabs, rel_rows)


def _rel_rows(rel_bias):
    n_head = rel_bias.shape[0]
    assert rel_bias.shape[1] == 2 * MAX_REL + 1 and n_head % ATT_HEADS == 0
    rb = rel_bias.astype(F32)
    far = rb[:, 2 * MAX_REL:]
    n_far = ATT_KW - 2 * MAX_REL
    row = jnp.concatenate([jnp.broadcast_to(far, (n_head, n_far)), rb[:, :0:-1],
                           jnp.broadcast_to(far, (n_head, REL_ROW - ATT_KW))], axis=1)
    return row.reshape(n_head // ATT_HEADS, ATT_HEADS, REL_ROW)


def _out_kernel(x_ref, a_ref, u_ref, v_ref, ws_ref, bs_ref, ga_ref, gb_ref, gt_ref, post_ref, w_ref, o_ref):
    n_slab = a_ref.shape[0]
    width = n_slab * LANES
    tm = x_ref.shape[0]
    ti = lax.broadcasted_iota(jnp.int32, (GATE_BLOCK, GATE_BLOCK), 0)
    si = lax.broadcasted_iota(jnp.int32, (GATE_BLOCK, GATE_BLOCK), 1)
    causal = (ti // CHUNK) >= (si // CHUNK)
    w_gate = [jnp.where(causal, ws_ref[g], 0.0).astype(BF16) for g in range(n_slab)]

    def gated(r):
        blocks = [slice(s, s + GATE_BLOCK) for s in range(r.start, r.stop, GATE_BLOCK)]
        groups = []
        for g in range(n_slab):
            v_wide = jnp.concatenate([v_ref[g, blk, :] for blk in blocks], axis=1)
            z = jnp.dot(w_gate[g], v_wide, preferred_element_type=F32)
            groups.append(jnp.concatenate(
                [u_ref[g, blk, :].astype(F32) * (z[:, n * LANES:(n + 1) * LANES] + bs_ref[g])
                 for n, blk in enumerate(blocks)], axis=0))
        return groups

    def normed(values, g_ref):
        ss = sum(jnp.sum(v * v, axis=-1, keepdims=True) for v in values)
        inv = lax.rsqrt(ss * (1.0 / width) + EPS)
        return [(v * inv * g_ref[:, p * LANES:(p + 1) * LANES]).astype(BF16) for p, v in enumerate(values)]

    coef = gt_ref[0] * post_ref[...]
    halves = (slice(0, tm // 2), slice(tm // 2, tm))
    gates = [gated(r) for r in halves]
    for r, gate in zip(halves, gates):
        attn = [a_ref[p, r, :].astype(F32) for p in range(n_slab)]
        merged = jnp.concatenate(normed(attn, ga_ref) + normed(gate, gb_ref), axis=1)
        y = jnp.dot(merged, w_ref[...], preferred_element_type=F32)
        o_ref[r, :] = x_ref[r, :] + coef * _rms(y)


def _out_proj(x2, seq, a_slabs, slabs, w_s, b_s, g_a, g_b, gt, post_g, w_out):
    t, d = x2.shape
    n_slab = a_slabs.shape[0]
    assert w_s.shape == (n_slab, GATE_BLOCK, GATE_BLOCK)
    tm = min(OUT_TM, seq)
    per_b = seq // tm
    b_bc = jnp.broadcast_to(b_s.astype(F32)[:, :, None], (n_slab, GATE_BLOCK, LANES))
    half_spec = pl.BlockSpec((1, n_slab * LANES), lambda i: (0, 0))
    const3 = lambda i: (0, 0, 0)
    return pl.pallas_call(
        _out_kernel,
        grid=(t // tm,),
        in_specs=[pl.BlockSpec((tm, d), lambda i: (i, 0)),
                  pl.BlockSpec((n_slab, tm, LANES), lambda i: (0, i, 0)),
                  pl.BlockSpec((n_slab, tm, LANES), lambda i: (3, i, 0)),
                  pl.BlockSpec((n_slab, tm, LANES), lambda i: (4, i, 0)),
                  pl.BlockSpec((n_slab, GATE_BLOCK, GATE_BLOCK), const3),
                  pl.BlockSpec((n_slab, GATE_BLOCK, LANES), const3),
                  half_spec, half_spec,
                  pl.BlockSpec((1, 1, d), lambda i: (i // per_b, 0, 0)),
                  pl.BlockSpec((1, d), lambda i: (0, 0)),
                  pl.BlockSpec((d, d), lambda i: (0, 0), pipeline_mode=pl.Buffered(1))],
        out_specs=pl.BlockSpec((tm, d), lambda i: (i, 0)),
        out_shape=jax.ShapeDtypeStruct((t, d), F32),
        compiler_params=_params(("parallel",)),
        name="gate_out_proj",
    )(x2, a_slabs, slabs, slabs, w_s, b_bc, g_a, g_b, gt, post_g, w_out)


def kernel(x, c, w_ada, b_ada, ffn1_pre_g, ffn1_post_g, ffn1_w_gu, ffn1_w_down, mix_pre_g, mix_post_g, w_in, rel_bias, ln_v_g, ln_v_b, w_s, b_s, g_out_a, g_out_b, w_out, ffn2_pre_g, ffn2_post_g, ffn2_w_gu, ffn2_w_down):
    bsz, seq, d = x.shape
    depth = w_ada.shape[0]
    assert seq % ATT_QB == 0 and seq % min(PROJ_TM, seq) == 0
    x2 = x.reshape(bsz * seq, d)
    row = lambda v: v.reshape(1, -1)
    for l in range(depth):
        mod = _ada(c, w_ada[l], b_ada[l]).reshape(bsz, N_MOD, 1, d)
        sh1, sc1, gt1, sh2, sc2, gt2, sh3, sc3, gt3 = [mod[:, i] for i in range(N_MOD)]

        jobs = (("gu", ffn2_w_gu[l]), ("cast", ffn2_w_down[l]), ("cast", w_in[l]), ("cast", w_out[l]))
        x2, (wgu2, wd2, w_in_bf, w_out_bf) = _ffn(
            x2, seq, sh1, sc1, gt1, row(ffn1_pre_g[l]), row(ffn1_post_g[l]),
            _pack_gu(ffn1_w_gu[l], FFN_TF), ffn1_w_down[l].astype(BF16), jobs)

        slabs = _proj(x2, seq, sh2, sc2, row(mix_pre_g[l]), row(ln_v_g[l]), row(ln_v_b[l]), w_in_bf)
        out_a = _attn(slabs, _rel_rows(rel_bias[l]), bsz, seq)
        x2 = _out_proj(x2, seq, out_a, slabs, w_s[l], b_s[l], row(g_out_a[l]), row(g_out_b[l]), gt2,
                       row(mix_post_g[l]), w_out_bf)

        x2, _ = _ffn(x2, seq, sh3, sc3, gt3, row(ffn2_pre_g[l]), row(ffn2_post_g[l]), wgu2, wd2)
    return x2.reshape(bsz, seq, d)
```

```python
import functools
import math

import jax
import jax.numpy as jnp
from jax import lax
from jax.experimental import pallas as pl
from jax.experimental.pallas import tpu as pltpu

F32 = jnp.float32
BF16 = jnp.bfloat16

EPS = 1e-6
CHUNK = 64
LEFT_CHUNKS = 8
HEAD_DIM = 64
MAX_REL = 128
GATE_BLOCK = 128
N_MOD = 9

LANES = 128
BF16_SUBLANES = 16
F32_SUBLANES = 8
PACK_ROWS = 128
ADA_TN = 1024
VMEM_LIMIT = 58 * 1024 * 1024

FFN_TM = 1024
FFN_TF = 512
PROJ_TM = 512
PROJ_SECTIONS = 5
ATT_QB = 512
ATT_SUB = 128
ATT_HEADS = 16
ATT_KW = ATT_SUB + LEFT_CHUNKS * CHUNK
REL_ROW = ATT_KW + ATT_SUB
OUT_TM = 512
NEG = -1e30
LOG2E = math.log2(math.e)


def _params(sem, vmem=VMEM_LIMIT):
    return pltpu.CompilerParams(dimension_semantics=sem, vmem_limit_bytes=vmem)


def _rms(v):
    return v * lax.rsqrt(jnp.mean(v * v, axis=-1, keepdims=True) + EPS)


def _modulated(x, gain, scale, shift):
    return _rms(x) * (gain * (1.0 + scale)) + shift


def _gelu(v):
    return 0.5 * v * (1.0 + lax.erf(v * (1.0 / math.sqrt(2.0))))


def _ada_kernel(c_ref, w_ref, b_ref, o_ref):
    c = c_ref[...]
    ca = (c * jax.nn.sigmoid(c)).astype(BF16)
    o_ref[...] = jnp.dot(ca, w_ref[...].astype(BF16), preferred_element_type=F32) + b_ref[...]


def _ada(c, w, b):
    bsz, d = c.shape
    n = w.shape[1]
    rows = F32_SUBLANES
    assert bsz <= rows and n % ADA_TN == 0
    cp = jnp.zeros((rows, d), F32).at[:bsz].set(c)
    tn = ADA_TN
    out = pl.pallas_call(
        _ada_kernel,
        grid=(n // tn,),
        in_specs=[pl.BlockSpec((rows, d), lambda j: (0, 0)),
                  pl.BlockSpec((d, tn), lambda j: (0, j)),
                  pl.BlockSpec((1, tn), lambda j: (0, j))],
        out_specs=pl.BlockSpec((rows, tn), lambda j: (0, j)),
        out_shape=jax.ShapeDtypeStruct((rows, n), F32),
        compiler_params=_params(("arbitrary",)),
        name="adaln",
    )(cp, w, b.reshape(1, n))
    return out[:bsz]


def _pack_gu_rows(src_ref, dst_ref):
    n_f, _, two_tf = dst_ref.shape
    tf = two_tf // 2
    d_ff = src_ref.shape[1] // 2
    for f in range(n_f):
        w = min(tf, d_ff - f * tf)
        dst_ref[f, :, :w] = src_ref[:, f * tf:f * tf + w].astype(BF16)
        dst_ref[f, :, w:2 * w] = src_ref[:, d_ff + f * tf:d_ff + f * tf + w].astype(BF16)
        if w < tf:
            dst_ref[f, :, 2 * w:] = jnp.zeros((dst_ref.shape[1], two_tf - 2 * w), BF16)


def _pack_ffn_kernel(gu_ref, wd_ref, gu_out_ref, wd_out_ref):
    _pack_gu_rows(gu_ref, gu_out_ref)
    wd_out_ref[...] = wd_ref[...].astype(BF16)


def _pack_ffn(w_gu, w_down, tf):
    d, two_f = w_gu.shape
    d_ff = w_down.shape[0]
    n_f = pl.cdiv(d_ff, tf)
    assert d_ff % PACK_ROWS == 0
    n_steps = d_ff // PACK_ROWS
    rb = _job_rows(d, n_steps)
    last = d // rb - 1
    return pl.pallas_call(
        _pack_ffn_kernel,
        grid=(n_steps,),
        in_specs=[pl.BlockSpec((rb, two_f), lambda r: (jnp.minimum(r, last), 0)),
                  pl.BlockSpec((PACK_ROWS, d), lambda r: (r, 0))],
        out_specs=[pl.BlockSpec((n_f, rb, 2 * tf), lambda r: (0, jnp.minimum(r, last), 0)),
                   pl.BlockSpec((PACK_ROWS, d), lambda r: (r, 0))],
        out_shape=[jax.ShapeDtypeStruct((n_f, d, 2 * tf), BF16),
                   jax.ShapeDtypeStruct((d_ff, d), BF16)],
        compiler_params=_params(("arbitrary",)),
        name="pack_ffn",
    )(w_gu, w_down)


def _ffn_kernel(*refs, last_width, job_kinds):
    n_job = len(job_kinds)
    x_ref, sh_ref, sc_ref, gt_ref, pre_ref, post_ref, wgu_ref, wd_ref = refs[:8]
    job_src = refs[8:8 + n_job]
    o_ref = refs[8 + n_job]
    job_dst = refs[9 + n_job:9 + 2 * n_job]
    (h_ref,) = refs[9 + 2 * n_job:]
    acc_ref = o_ref
    step = pl.program_id(1)
    n_f = pl.num_programs(1)
    tf = wd_ref.shape[0]
    is_partial = _ffn_tile(pl.program_id(0), step, n_f) == n_f - 1

    def side_jobs():
        for kind, src, dst in zip(job_kinds, job_src, job_dst):
            if kind == "gu":
                _pack_gu_rows(src, dst)
            else:
                dst[...] = src[...].astype(BF16)

    def down_input(h, width):
        gu = jnp.dot(h, wgu_ref[0, :, :2 * width], preferred_element_type=F32)
        g, u = gu[:, :width], gu[:, width:]
        return (g * jax.nn.sigmoid(g) * u).astype(BF16)

    tm = x_ref.shape[0]
    halves = (slice(0, tm // 2), slice(tm // 2, tm))

    def accumulate(width):
        acts = [down_input(h_ref[r, :], width) for r in halves]
        for r, a in zip(halves, acts):
            acc_ref[r, :] += jnp.dot(a, wd_ref[:width, :], preferred_element_type=F32)

    @pl.when(step == 0)
    def _():
        for r in halves:
            h = _modulated(x_ref[r, :], pre_ref[...], sc_ref[0], sh_ref[0]).astype(BF16)
            h_ref[r, :] = h
            acc_ref[r, :] = jnp.dot(down_input(h, tf), wd_ref[...], preferred_element_type=F32)
        side_jobs()

    @pl.when((step > 0) & (step < n_f - 1) & jnp.logical_not(is_partial))
    def _():
        accumulate(tf)
        side_jobs()

    @pl.when(is_partial)
    def _():
        accumulate(last_width)
        side_jobs()

    @pl.when(step == n_f - 1)
    def _():
        for r in halves:
            a = down_input(h_ref[r, :], tf)
            y = acc_ref[r, :] + jnp.dot(a, wd_ref[...], preferred_element_type=F32)
            o_ref[r, :] = x_ref[r, :] + (0.5 * gt_ref[0] * post_ref[...]) * _rms(y)
        side_jobs()


def _ffn_tile(i, step, n_f):
    mid = (n_f - 1) // 2
    pos = jnp.where(i % 2 == 0, step, n_f - 1 - step)
    return jnp.where(pos == mid, n_f - 1, jnp.where(pos < mid, pos, pos - 1))


def _job_rows(n_rows, n_steps):
    rb = BF16_SUBLANES
    while n_rows % rb or n_rows // rb > n_steps:
        rb += BF16_SUBLANES
    return rb


def _ffn(x2, seq, sh, sc, gt, pre_g, post_g, wgu, wd, jobs=()):
    t, d = x2.shape
    d_ff = wd.shape[0]
    tm, tf = min(FFN_TM, seq), wgu.shape[2] // 2
    n_f = wgu.shape[0]
    assert n_f >= 3 and n_f == pl.cdiv(d_ff, tf)
    n_steps = (t // tm) * n_f
    per_b = seq // tm
    mod_spec = pl.BlockSpec((1, 1, d), lambda i, f: (i // per_b, 0, 0))
    vec_spec = pl.BlockSpec((1, d), lambda i, f: (0, 0))

    job_in, job_out, job_shape = [], [], []
    for kind, w in jobs:
        rows, cols = w.shape
        rb = _job_rows(rows, n_steps)
        blk = lambda i, f, n=rows // rb: jnp.minimum(i * n_f + f, n - 1)
        job_in.append(pl.BlockSpec((rb, cols), lambda i, f, blk=blk: (blk(i, f), 0)))
        if kind == "gu":
            job_out.append(pl.BlockSpec((n_f, rb, 2 * tf), lambda i, f, blk=blk: (0, blk(i, f), 0)))
            job_shape.append(jax.ShapeDtypeStruct((n_f, rows, 2 * tf), BF16))
        else:
            job_out.append(pl.BlockSpec((rb, cols), lambda i, f, blk=blk: (blk(i, f), 0)))
            job_shape.append(jax.ShapeDtypeStruct((rows, cols), BF16))

    out = pl.pallas_call(
        functools.partial(_ffn_kernel, last_width=d_ff - (n_f - 1) * tf,
                          job_kinds=tuple(kind for kind, _ in jobs)),
        grid=(t // tm, n_f),
        in_specs=[pl.BlockSpec((tm, d), lambda i, f: (i, 0)),
                  mod_spec, mod_spec, mod_spec, vec_spec, vec_spec,
                  pl.BlockSpec((1, d, 2 * tf), lambda i, f: (_ffn_tile(i, f, n_f), 0, 0)),
                  pl.BlockSpec((tf, d), lambda i, f: (_ffn_tile(i, f, n_f), 0))] + job_in,
        out_specs=[pl.BlockSpec((tm, d), lambda i, f: (i, 0))] + job_out,
        out_shape=[jax.ShapeDtypeStruct((t, d), F32)] + job_shape,
        scratch_shapes=[pltpu.VMEM((tm, d), BF16)],
        compiler_params=_params(("arbitrary", "arbitrary")),
        name="ffn_convert" if jobs else "ffn",
    )(x2, sh, sc, gt, pre_g, post_g, wgu, wd, *[w for _, w in jobs])
    return out[0], out[1:]


def _proj_kernel(x_ref, sh_ref, sc_ref, pre_ref, lng_ref, lnb_ref, w_ref, o_ref):
    n_slab = o_ref.shape[0] // PROJ_SECTIONS
    tn = n_slab * LANES
    h = _modulated(x_ref[...], pre_ref[...], sc_ref[0], sh_ref[0]).astype(BF16)

    def emit(j, epilogue):
        r = jnp.dot(h, w_ref[:, j * tn:(j + 1) * tn], preferred_element_type=F32)
        val = epilogue(r).astype(BF16)
        for p in range(n_slab):
            o_ref[j * n_slab + p] = val[:, p * LANES:(p + 1) * LANES]

    def gelu_layer_norm(v):
        ge = _gelu(v)
        xc = ge - jnp.mean(ge, axis=-1, keepdims=True)
        y = xc * lax.rsqrt(jnp.mean(xc * xc, axis=-1, keepdims=True) + EPS)
        return y * lng_ref[...] + lnb_ref[...]

    emit(4, gelu_layer_norm)
    emit(1, lambda v: v)
    emit(3, _gelu)
    emit(0, lambda v: v * (LOG2E / math.sqrt(HEAD_DIM)))
    emit(2, lambda v: v)


def _proj(x2, seq, sh, sc, pre_g, ln_g, ln_b, w_in):
    t, d = x2.shape
    n = w_in.shape[1]
    tn = n // PROJ_SECTIONS
    n_slab = tn // LANES
    tm = min(PROJ_TM, seq)
    per_b = seq // tm
    mod_spec = pl.BlockSpec((1, 1, d), lambda i: (i // per_b, 0, 0))
    vec_spec = pl.BlockSpec((1, d), lambda i: (0, 0))
    ln_spec = pl.BlockSpec((1, tn), lambda i: (0, 0))
    return pl.pallas_call(
        _proj_kernel,
        grid=(t // tm,),
        in_specs=[pl.BlockSpec((tm, d), lambda i: (i, 0)),
                  mod_spec, mod_spec, vec_spec, ln_spec, ln_spec,
                  pl.BlockSpec((d, n), lambda i: (0, 0), pipeline_mode=pl.Buffered(1))],
        out_specs=pl.BlockSpec((PROJ_SECTIONS * n_slab, tm, LANES), lambda i: (0, i, 0)),
        out_shape=jax.ShapeDtypeStruct((PROJ_SECTIONS * n_slab, t, LANES), BF16),
        compiler_params=_params(("parallel",)),
        name="in_proj",
    )(x2, sh, sc, pre_g, ln_g, ln_b, w_in)


def _attn_kernel(q_ref, kp_ref, kc_ref, vp_ref, vc_ref, rel_ref, o_ref, tab_ref):
    n_sub = ATT_QB // ATT_SUB
    n_slab = q_ref.shape[0]
    n_head = tab_ref.shape[1]

    @pl.when((pl.program_id(1) == 0) & (pl.program_id(2) == 0))
    def _():
        r = lax.broadcasted_iota(jnp.int32, (ATT_SUB, ATT_KW), 0)
        c = lax.broadcasted_iota(jnp.int32, (ATT_SUB, ATT_KW), 1)
        lo = (r // CHUNK) * CHUNK
        in_band = (c >= lo) & (c < lo + (LEFT_CHUNKS + 1) * CHUNK)
        for h in range(n_head):
            row = (rel_ref[0, h:h + 1, :] - rel_ref[0, h:h + 1, 0:1]) * LOG2E
            base = jnp.broadcast_to(row, (ATT_SUB, REL_ROW))
            toe = pltpu.roll(base, 0, 1, stride=1, stride_axis=0)[:, :ATT_KW]
            t0 = jnp.where(in_band, toe, NEG)
            tab_ref[0, h] = t0
            for i in range(n_sub):
                tab_ref[1 + i, h] = jnp.where(c + i * ATT_SUB >= ATT_QB, t0, NEG)

    qblk = pl.program_id(2)

    pair = LANES // HEAD_DIM
    upper = lax.broadcasted_iota(jnp.int32, (ATT_SUB, LANES), 1) >= HEAD_DIM
    ahead = 2
    near = ATT_KW - 2 * MAX_REL

    def attend_slab(slab, sequence_start):
        q = q_ref[slab]
        k = jnp.concatenate([kp_ref[slab], kc_ref[slab]], axis=0)
        v = jnp.concatenate([vp_ref[slab], vc_ref[slab]], axis=0)
        heads = slice(slab * pair, (slab + 1) * pair)

        def scores(i):
            qs = q[i * ATT_SUB:(i + 1) * ATT_SUB].astype(F32)
            qh = jnp.concatenate([jnp.where(upper, 0.0, qs), jnp.where(upper, qs, 0.0)], axis=0)
            ks = k[i * ATT_SUB:i * ATT_SUB + ATT_KW]
            return lax.dot_general(qh.astype(BF16), ks, (((1,), (1,)), ((), ())), preferred_element_type=F32)

        pending = [scores(i) for i in range(ahead)]
        for i in range(n_sub):
            r0 = i * ATT_SUB
            vs = v[r0:r0 + ATT_KW]
            s = pending.pop(0).reshape(pair, ATT_SUB, ATT_KW)
            if i + ahead < n_sub:
                pending.append(scores(i + ahead))
            if sequence_start:
                s = s + tab_ref[1 + i, heads]
            else:
                s = jnp.concatenate([s[..., :LANES] + tab_ref[0, heads, :, :LANES], s[..., LANES:near],
                                     s[..., near:] + tab_ref[0, heads, :, near:]], axis=-1)
            m = jnp.max(s, axis=-1, keepdims=True)
            p = jnp.exp2(s - m)
            l = jnp.sum(p, axis=-1, keepdims=True)
            pv = jnp.dot(p.reshape(pair * ATT_SUB, ATT_KW).astype(BF16), vs, preferred_element_type=F32)
            pv = pv.reshape(pair, ATT_SUB, LANES) / l
            o_ref[slab, r0:r0 + ATT_SUB, :] = jnp.where(upper, pv[1], pv[0]).astype(o_ref.dtype)

    def attend(sequence_start):
        for slab in range(n_slab):
            attend_slab(slab, sequence_start)

    @pl.when(qblk == 0)
    def _():
        attend(True)

    @pl.when(qblk > 0)
    def _():
        attend(False)


def _attn(slabs, rel_rows, bsz, seq):
    n_grp, n_head = rel_rows.shape[:2]
    n_slab = n_head * HEAD_DIM // LANES
    t = slabs.shape[1]
    nq = seq // ATT_QB
    blk = (n_slab, ATT_QB, LANES)

    def cur(section):
        return lambda p, b, i: (section * n_grp + p, b * nq + i, 0)

    def prev(section):
        return lambda p, b, i: (section * n_grp + p, b * nq + jnp.maximum(i - 1, 0), 0)

    return pl.pallas_call(
        _attn_kernel,
        grid=(n_grp, bsz, nq),
        in_specs=[pl.BlockSpec(blk, cur(0)),
                  pl.BlockSpec(blk, prev(1)), pl.BlockSpec(blk, cur(1)),
                  pl.BlockSpec(blk, prev(2)), pl.BlockSpec(blk, cur(2)),
                  pl.BlockSpec((1, n_head, REL_ROW), lambda p, b, i: (p, 0, 0))],
        out_specs=pl.BlockSpec(blk, lambda p, b, i: (p, b * nq + i, 0)),
        out_shape=jax.ShapeDtypeStruct((n_grp * n_slab, t, LANES), BF16),
        scratch_shapes=[pltpu.VMEM((1 + ATT_QB // ATT_SUB, n_head, ATT_SUB, ATT_KW), F32)],
        compiler_params=_params(("arbitrary", "arbitrary", "arbitrary")),
        name="chunk_attn",
    )(slabs, slabs, slabs, slabs, slabs, rel_rows)


def _rel_rows(rel_bias):
    n_head = rel_bias.shape[0]
    assert rel_bias.shape[1] == 2 * MAX_REL + 1 and n_head % ATT_HEADS == 0
    rb = rel_bias.astype(F32)
    far = rb[:, 2 * MAX_REL:]
    n_far = ATT_KW - 2 * MAX_REL
    row = jnp.concatenate([jnp.broadcast_to(far, (n_head, n_far)), rb[:, :0:-1],
                           jnp.broadcast_to(far, (n_head, REL_ROW - ATT_KW))], axis=1)
    return row.reshape(n_head // ATT_HEADS, ATT_HEADS, REL_ROW)


def _out_kernel(x_ref, a_ref, u_ref, v_ref, ws_ref, bs_ref, ga_ref, gb_ref, gt_ref, post_ref, w_ref, o_ref):
    n_slab = a_ref.shape[0]
    width = n_slab * LANES
    tm = x_ref.shape[0]
    ti = lax.broadcasted_iota(jnp.int32, (GATE_BLOCK, GATE_BLOCK), 0)
    si = lax.broadcasted_iota(jnp.int32, (GATE_BLOCK, GATE_BLOCK), 1)
    causal = (ti // CHUNK) >= (si // CHUNK)
    w_gate = [jnp.where(causal, ws_ref[g], 0.0).astype(BF16) for g in range(n_slab)]

    def gated(r):
        blocks = [slice(s, s + GATE_BLOCK) for s in range(r.start, r.stop, GATE_BLOCK)]
        groups = []
        for g in range(n_slab):
            v_wide = jnp.concatenate([v_ref[g, blk, :] for blk in blocks], axis=1)
            z = jnp.dot(w_gate[g], v_wide, preferred_element_type=F32)
            groups.append(jnp.concatenate(
                [u_ref[g, blk, :].astype(F32) * (z[:, n * LANES:(n + 1) * LANES] + bs_ref[g])
                 for n, blk in enumerate(blocks)], axis=0))
        return groups

    def normed(values, g_ref):
        ss = sum(jnp.sum(v * v, axis=-1, keepdims=True) for v in values)
        inv = lax.rsqrt(ss * (1.0 / width) + EPS)
        return [(v * inv * g_ref[:, p * LANES:(p + 1) * LANES]).astype(BF16) for p, v in enumerate(values)]

    coef = gt_ref[0] * post_ref[...]
    halves = (slice(0, tm // 2), slice(tm // 2, tm))
    gates = [gated(r) for r in halves]
    for r, gate in zip(halves, gates):
        attn = [a_ref[p, r, :].astype(F32) for p in range(n_slab)]
        merged = jnp.concatenate(normed(attn, ga_ref) + normed(gate, gb_ref), axis=1)
        y = jnp.dot(merged, w_ref[...], preferred_element_type=F32)
        o_ref[r, :] = x_ref[r, :] + coef * _rms(y)


def _out_proj(x2, seq, a_slabs, slabs, w_s, b_s, g_a, g_b, gt, post_g, w_out):
    t, d = x2.shape
    n_slab = a_slabs.shape[0]
    assert w_s.shape == (n_slab, GATE_BLOCK, GATE_BLOCK)
    tm = min(OUT_TM, seq)
    per_b = seq // tm
    b_bc = jnp.broadcast_to(b_s.astype(F32)[:, :, None], (n_slab, GATE_BLOCK, LANES))
    half_spec = pl.BlockSpec((1, n_slab * LANES), lambda i: (0, 0))
    const3 = lambda i: (0, 0, 0)
    return pl.pallas_call(
        _out_kernel,
        grid=(t // tm,),
        in_specs=[pl.BlockSpec((tm, d), lambda i: (i, 0)),
                  pl.BlockSpec((n_slab, tm, LANES), lambda i: (0, i, 0)),
                  pl.BlockSpec((n_slab, tm, LANES), lambda i: (3, i, 0)),
                  pl.BlockSpec((n_slab, tm, LANES), lambda i: (4, i, 0)),
                  pl.BlockSpec((n_slab, GATE_BLOCK, GATE_BLOCK), const3),
                  pl.BlockSpec((n_slab, GATE_BLOCK, LANES), const3),
                  half_spec, half_spec,
                  pl.BlockSpec((1, 1, d), lambda i: (i // per_b, 0, 0)),
                  pl.BlockSpec((1, d), lambda i: (0, 0)),
                  pl.BlockSpec((d, d), lambda i: (0, 0), pipeline_mode=pl.Buffered(1))],
        out_specs=pl.BlockSpec((tm, d), lambda i: (i, 0)),
        out_shape=jax.ShapeDtypeStruct((t, d), F32),
        compiler_params=_params(("parallel",)),
        name="gate_out_proj",
    )(x2, a_slabs, slabs, slabs, w_s, b_bc, g_a, g_b, gt, post_g, w_out)


def kernel(x, c, w_ada, b_ada, ffn1_pre_g, ffn1_post_g, ffn1_w_gu, ffn1_w_down, mix_pre_g, mix_post_g, w_in, rel_bias, ln_v_g, ln_v_b, w_s, b_s, g_out_a, g_out_b, w_out, ffn2_pre_g, ffn2_post_g, ffn2_w_gu, ffn2_w_down):
    bsz, seq, d = x.shape
    depth = w_ada.shape[0]
    assert seq % ATT_QB == 0 and seq % min(PROJ_TM, seq) == 0
    x2 = x.reshape(bsz * seq, d)
    row = lambda v: v.reshape(1, -1)
    for l in range(depth):
        mod = _ada(c, w_ada[l], b_ada[l]).reshape(bsz, N_MOD, 1, d)
        sh1, sc1, gt1, sh2, sc2, gt2, sh3, sc3, gt3 = [mod[:, i] for i in range(N_MOD)]

        jobs = (("gu", ffn2_w_gu[l]), ("cast", ffn2_w_down[l]), ("cast", w_in[l]), ("cast", w_out[l]))
        x2, (wgu2, wd2, w_in_bf, w_out_bf) = _ffn(
            x2, seq, sh1, sc1, gt1, row(ffn1_pre_g[l]), row(ffn1_post_g[l]),
            *_pack_ffn(ffn1_w_gu[l], ffn1_w_down[l], FFN_TF), jobs)

        slabs = _proj(x2, seq, sh2, sc2, row(mix_pre_g[l]), row(ln_v_g[l]), row(ln_v_b[l]), w_in_bf)
        out_a = _attn(slabs, _rel_rows(rel_bias[l]), bsz, seq)
        x2 = _out_proj(x2, seq, out_a, slabs, w_s[l], b_s[l], row(g_out_a[l]), row(g_out_b[l]), gt2,
                       row(mix_post_g[l]), w_out_bf)

        x2, _ = _ffn(x2, seq, sh3, sc3, gt3, row(ffn2_pre_g[l]), row(ffn2_post_g[l]), wgu2, wd2)
    return x2.reshape(bsz, seq, d)
```
